```python
import jax
import jax.numpy as jnp
from jax import lax
import numpy as np

D_MODEL = 1024
BATCH = 8
SEQ = 2048
DEPTH = 2

N_META = 16
RMS_EPS = 1e-6
LN_EPS = 1e-5
D_A = D_MODEL // 2
D_B = D_MODEL // 2
CONV_A_WIDTH = 31
CONV_B_WIDTH = 3
EVEN_COLS = 2 * D_A + 3 * D_B
HEAD_DIM = 64
N_Q_HEADS = 8
N_KV_HEADS = 2
GQA_GROUP = N_Q_HEADS // N_KV_HEADS
D_ATT = N_Q_HEADS * HEAD_DIM
D_KV = N_KV_HEADS * HEAD_DIM
WINDOW = 128
BLOCK = 128
ROPE_THETA = 10000.0
RWKV_HEAD = 64
D_R = D_MODEL // 2
N_R_HEADS = D_R // RWKV_HEAD
LORA_W = 64
LORA_A = 64
LORA_G = 128
RWKV_GN_EPS = 64e-5
ATT_COLS = D_ATT + 2 * D_KV
RWKV_COLS = 3 * D_R + LORA_W + LORA_A + LORA_G
ODD_COLS = ATT_COLS + RWKV_COLS
D_FF = 2816
FF_CONV_WIDTH = 3
NEG_INF = -1e30

kernel_name = 'hybrid_conv_swa_rwkv7_block'


def rms_norm(x, g):
    xf = x.astype(jnp.float32)
    y = xf * lax.rsqrt(jnp.mean(xf * xf, axis=-1, keepdims=True) + RMS_EPS)
    return (y * g.astype(jnp.float32)).astype(x.dtype)


def layer_norm(x, g, b):
    xf = x.astype(jnp.float32)
    mu = jnp.mean(xf, axis=-1, keepdims=True)
    var = jnp.mean(jnp.square(xf - mu), axis=-1, keepdims=True)
    y = (xf - mu) * lax.rsqrt(var + LN_EPS)
    return (y * g.astype(jnp.float32) + b.astype(jnp.float32)).astype(x.dtype)


def causal_dwconv(x, w):
    k_width, ch = w.shape
    return lax.conv_general_dilated(
        x, w[:, None, :].astype(x.dtype), window_strides=(1,),
        padding=[(k_width - 1, 0)], dimension_numbers=('NWC', 'WIO', 'NWC'),
        feature_group_count=ch)


def rope(x, pos):
    half = x.shape[-1] // 2
    inv = ROPE_THETA ** (-jnp.arange(half, dtype=jnp.float32) / half)
    ang = pos.astype(jnp.float32)[:, None] * inv[None, :]
    cos = jnp.cos(ang)[None, :, None, :]
    sin = jnp.sin(ang)[None, :, None, :]
    xf = x.astype(jnp.float32)
    x1, x2 = xf[..., :half], xf[..., half:]
    return jnp.concatenate([x1 * cos - x2 * sin, x2 * cos + x1 * sin], axis=-1).astype(x.dtype)


def conformer_conv_group(a_val, a_gate, conv_w, ln_g, ln_b):
    u = a_val * jax.nn.sigmoid(a_gate)
    u = causal_dwconv(u, conv_w)
    return jax.nn.silu(layer_norm(u, ln_g, ln_b))


def short_conv_group(gate_b, gate_c, x_in, conv_w):
    return gate_b * causal_dwconv(gate_c * x_in, conv_w)


def even_mixer(h, w_in, conv_a, ln_a_g, ln_a_b, conv_b, w_out):
    p = h @ w_in
    a_val, a_gate, g_b, g_c, x_in = jnp.split(
        p, [D_A, 2 * D_A, 2 * D_A + D_B, 2 * D_A + 2 * D_B], axis=-1)
    y = jnp.concatenate([conformer_conv_group(a_val, a_gate, conv_a, ln_a_g, ln_a_b),
                         short_conv_group(g_b, g_c, x_in, conv_b)], axis=-1)
    return y @ w_out


def swa_sink_attention(q, k, v, sinks):
    bsz, t_len = q.shape[:2]
    pad = BLOCK - N_META
    t_pad = t_len + pad
    nb = t_pad // BLOCK
    padt = lambda z: jnp.pad(z, ((0, 0), (pad, 0), (0, 0), (0, 0)))
    qb = padt(q).reshape(bsz, nb, BLOCK, N_KV_HEADS, GQA_GROUP, HEAD_DIM)
    kb = padt(k).reshape(bsz, nb, BLOCK, N_KV_HEADS, HEAD_DIM)
    vb = padt(v).reshape(bsz, nb, BLOCK, N_KV_HEADS, HEAD_DIM)

    def band(z):
        prev = jnp.pad(z, ((0, 0), (1, 0), (0, 0), (0, 0), (0, 0)))[:, :-1]
        return jnp.concatenate([prev, z], axis=2)

    k_band, v_band = band(kb), band(vb)
    k_meta, v_meta = k[:, :N_META], v[:, :N_META]
    scale = HEAD_DIM ** -0.5
    s_band = jnp.einsum('bnqkgd,bnskd->bnkgqs', qb, k_band).astype(jnp.float32) * scale
    s_meta = jnp.einsum('bnqkgd,bmkd->bnkgqm', qb, k_meta).astype(jnp.float32) * scale

    blk0 = jnp.arange(nb)[:, None] * BLOCK
    t_pos = blk0 + jnp.arange(BLOCK)[None, :]
    s_pos = blk0 - BLOCK + jnp.arange(2 * BLOCK)[None, :]
    dist = t_pos[:, :, None] - s_pos[:, None, :]
    band_ok = (s_pos[:, None, :] >= BLOCK) & (dist >= 0) & (dist < WINDOW)
    meta_ok = (pad + jnp.arange(N_META))[None, None, :] <= t_pos[:, :, None]
    s_band = jnp.where(band_ok[None, :, None, None], s_band, NEG_INF)
    s_meta = jnp.where(meta_ok[None, :, None, None], s_meta, NEG_INF)
    s_sink = jnp.broadcast_to(
        sinks.astype(jnp.float32).reshape(1, 1, N_KV_HEADS, GQA_GROUP, 1, 1),
        s_band.shape[:-1] + (1,))
    prob = jax.nn.softmax(jnp.concatenate([s_band, s_meta, s_sink], axis=-1), axis=-1)
    p_band = prob[..., :2 * BLOCK].astype(v.dtype)
    p_meta = prob[..., 2 * BLOCK:2 * BLOCK + N_META].astype(v.dtype)
    out = (jnp.einsum('bnkgqs,bnskd->bnqkgd', p_band, v_band)
           + jnp.einsum('bnkgqm,bmkd->bnqkgd', p_meta, v_meta))
    return out.reshape(bsz, t_pad, D_ATT)[:, pad:]


def wkv7_scan(r, w, k, v, a, b):
    bsz, _, nh, n = r.shape

    def step(s, inp):
        r_t, w_t, k_t, v_t, a_t, b_t = inp
        sa = jnp.einsum('bhij,bhj->bhi', s, a_t)
        s = s * w_t[:, :, None, :] + sa[..., None] * b_t[:, :, None, :] + v_t[..., None] * k_t[:, :, None, :]
        return s, jnp.einsum('bhij,bhj->bhi', s, r_t)

    xs = tuple(jnp.moveaxis(z, 1, 0) for z in (r, w, k, v, a, b))
    s0 = jnp.zeros((bsz, nh, n, n), jnp.float32)
    _, y = lax.scan(step, s0, xs)
    return jnp.moveaxis(y, 0, 1)


def rwkv7_group(pr, mu, w0, w2, a0, a2, g2, k_k, k_a, r_k, lnx_g, lnx_b):
    f32 = jnp.float32
    bsz, t_len, _ = pr.shape
    pr = pr.astype(f32)
    prev = jnp.pad(pr, ((0, 0), (1, 0), (0, 0)))[:, :-1]
    pr = pr + (prev - pr) * mu.astype(f32)
    r, k, v, wd, ad, gd = jnp.split(
        pr, [D_R, 2 * D_R, 3 * D_R, 3 * D_R + LORA_W, 3 * D_R + LORA_W + LORA_A], axis=-1)
    w_log = -jax.nn.softplus(-(w0.astype(f32) + jnp.tanh(wd) @ w2.astype(f32))) - 0.5
    decay = jnp.exp(-jnp.exp(w_log))
    alpha = jax.nn.sigmoid(a0.astype(f32) + ad @ a2.astype(f32))
    g = jax.nn.sigmoid(gd) @ g2.astype(f32)
    heads = lambda z: z.reshape(bsz, t_len, N_R_HEADS, RWKV_HEAD)
    kk = heads(k * k_k.astype(f32))
    kk = kk / jnp.maximum(jnp.sqrt(jnp.sum(kk * kk, axis=-1, keepdims=True)), 1e-12)
    k = k * (1.0 + (alpha - 1.0) * k_a.astype(f32))
    r_h, k_h, v_h, a_h = heads(r), heads(k), heads(v), heads(alpha)
    y = wkv7_scan(r_h, heads(decay), k_h, v_h, -kk, kk * a_h)
    mean = jnp.mean(y, axis=-1, keepdims=True)
    var = jnp.mean(jnp.square(y - mean), axis=-1, keepdims=True)
    y = ((y - mean) * lax.rsqrt(var + RWKV_GN_EPS)).reshape(bsz, t_len, D_R)
    y = y * lnx_g.astype(f32) + lnx_b.astype(f32)
    bonus = jnp.sum(r_h * k_h * r_k.astype(f32), axis=-1, keepdims=True) * v_h
    y = y + bonus.reshape(bsz, t_len, D_R)
    return y * g


def odd_mixer(h, w_in, sinks, mu, w0, w2, a0, a2, g2, k_k, k_a, r_k, lnx_g, lnx_b, w_out):
    bsz, t_len, _ = h.shape
    p = h @ w_in
    q = p[..., :D_ATT].reshape(bsz, t_len, N_Q_HEADS, HEAD_DIM)
    k = p[..., D_ATT:D_ATT + D_KV].reshape(bsz, t_len, N_KV_HEADS, HEAD_DIM)
    v = p[..., D_ATT + D_KV:ATT_COLS].reshape(bsz, t_len, N_KV_HEADS, HEAD_DIM)
    pos = jnp.arange(t_len)
    y_att = swa_sink_attention(rope(q, pos), rope(k, pos), v, sinks)
    y_rwkv = rwkv7_group(p[..., ATT_COLS:], mu, w0, w2, a0, a2, g2, k_k, k_a, r_k, lnx_g, lnx_b)
    y = jnp.concatenate([y_att.astype(h.dtype), y_rwkv.astype(h.dtype)], axis=-1)
    return y @ w_out


def conv_glu(h, w_up, conv_w, conv_b, w_down):
    u = h @ w_up
    gate, val = u[..., :D_FF], u[..., D_FF:]
    gate = causal_dwconv(gate, conv_w) + conv_b.astype(h.dtype)
    return (jax.nn.silu(gate) * val) @ w_down


def setup_inputs(seed: int = 0) -> dict:
    key = jax.random.key(seed)
    ks = iter(jax.random.split(key, 32))
    f32 = jnp.float32
    nrm = lambda shape, s: jax.random.normal(next(ks), shape, f32) * s
    uni = lambda shape, lo, hi: jax.random.uniform(next(ks), shape, f32, lo, hi)
    ne = (DEPTH + 1) // 2
    no = DEPTH // 2
    return {
        'x': nrm((BATCH, SEQ, D_MODEL), 1.0),
        'meta_tokens': nrm((N_META, D_MODEL), 1.0),
        'norm_mix': 1.0 + nrm((DEPTH, D_MODEL), 0.02),
        'norm_ffn': 1.0 + nrm((DEPTH, D_MODEL), 0.02),
        'norm_final': 1.0 + nrm((D_MODEL,), 0.02),
        'ev_w_in': nrm((ne, D_MODEL, EVEN_COLS), D_MODEL ** -0.5),
        'ev_conv_a': nrm((ne, CONV_A_WIDTH, D_A), CONV_A_WIDTH ** -0.5),
        'ev_ln_a_g': 1.0 + nrm((ne, D_A), 0.02),
        'ev_ln_a_b': nrm((ne, D_A), 0.02),
        'ev_conv_b': nrm((ne, CONV_B_WIDTH, D_B), CONV_B_WIDTH ** -0.5),
        'ev_w_out': nrm((ne, D_A + D_B, D_MODEL), (D_A + D_B) ** -0.5),
        'od_w_in': nrm((no, D_MODEL, ODD_COLS), D_MODEL ** -0.5),
        'od_sinks': nrm((no, N_Q_HEADS), 0.5),
        'od_mu': uni((no, RWKV_COLS), 0.0, 1.0),
        'od_w0': uni((no, D_R), -6.0, -1.0),
        'od_w2': nrm((no, LORA_W, D_R), 0.1),
        'od_a0': nrm((no, D_R), 0.1),
        'od_a2': nrm((no, LORA_A, D_R), 0.1),
        'od_g2': nrm((no, LORA_G, D_R), LORA_G ** -0.5),
        'od_k_k': 0.85 + nrm((no, D_R), 0.02),
        'od_k_a': 1.0 + nrm((no, D_R), 0.02),
        'od_r_k': nrm((no, N_R_HEADS, RWKV_HEAD), 0.1),
        'od_lnx_g': 1.0 + nrm((no, D_R), 0.02),
        'od_lnx_b': nrm((no, D_R), 0.02),
        'od_w_out': nrm((no, D_ATT + D_R, D_MODEL), (D_ATT + D_R) ** -0.5),
        'ff_w_up': nrm((DEPTH, D_MODEL, 2 * D_FF), D_MODEL ** -0.5),
        'ff_conv': nrm((DEPTH, FF_CONV_WIDTH, D_FF), FF_CONV_WIDTH ** -0.5),
        'ff_conv_b': nrm((DEPTH, D_FF), 0.02),
        'ff_w_down': nrm((DEPTH, D_FF, D_MODEL), D_FF ** -0.5),
    }


def reference(x, meta_tokens, norm_mix, norm_ffn, norm_final,
              ev_w_in, ev_conv_a, ev_ln_a_g, ev_ln_a_b, ev_conv_b, ev_w_out,
              od_w_in, od_sinks, od_mu, od_w0, od_w2, od_a0, od_a2, od_g2,
              od_k_k, od_k_a, od_r_k, od_lnx_g, od_lnx_b, od_w_out,
              ff_w_up, ff_conv, ff_conv_b, ff_w_down):
    bsz = x.shape[0]
    meta = jnp.broadcast_to(meta_tokens[None].astype(x.dtype), (bsz, N_META, D_MODEL))
    h = jnp.concatenate([meta, x], axis=1)
    for i in range(DEPTH):
        hn = rms_norm(h, norm_mix[i])
        j = i // 2
        if i % 2 == 0:
            h = h + even_mixer(hn, ev_w_in[j], ev_conv_a[j], ev_ln_a_g[j], ev_ln_a_b[j],
                               ev_conv_b[j], ev_w_out[j])
        else:
            h = h + odd_mixer(hn, od_w_in[j], od_sinks[j], od_mu[j], od_w0[j], od_w2[j],
                              od_a0[j], od_a2[j], od_g2[j], od_k_k[j], od_k_a[j], od_r_k[j],
                              od_lnx_g[j], od_lnx_b[j], od_w_out[j])
        h = h + conv_glu(rms_norm(h, norm_ffn[i]), ff_w_up[i], ff_conv[i], ff_conv_b[i], ff_w_down[i])
    return rms_norm(h, norm_final)[:, N_META:]
```

```python
import functools

import jax
import jax.numpy as jnp
from jax import lax
from jax.experimental import pallas as pl
from jax.experimental.pallas import tpu as pltpu

D_MODEL = 1024
N_META = 16
RMS_EPS = 1e-6
LN_EPS = 1e-5
D_A = 512
D_B = 512
CONV_A_WIDTH = 31
CONV_B_WIDTH = 3
HEAD_DIM = 64
N_Q_HEADS = 8
N_KV_HEADS = 2
GQA_GROUP = N_Q_HEADS // N_KV_HEADS
D_ATT = N_Q_HEADS * HEAD_DIM
D_KV = N_KV_HEADS * HEAD_DIM
WINDOW = 128
BLOCK = 128
ROPE_THETA = 10000.0
D_R = 512
N_R_HEADS = 8
RWKV_HEAD = 64
LORA_W = 64
LORA_A = 64
LORA_G = 128
RWKV_GN_EPS = 64e-5
ATT_COLS = D_ATT + 2 * D_KV
RWKV_COLS = 3 * D_R + LORA_W + LORA_A + LORA_G
D_FF = 2816
NEG_INF = -1e30

VMEM_LIMIT_BYTES = 56 * 1024 * 1024
LANES = 128
SUBLANES = 8
ROW_TILE = 688
CONV_ROWS = 16
FF_CHUNK = 256
WKV_CHUNK = 16
HEADS_PER_GROUP = LANES // RWKV_HEAD
N_GROUPS = N_R_HEADS // HEADS_PER_GROUP

F32 = jnp.float32
BF16 = jnp.bfloat16


def _const_spec(shape):
    nd = len(shape)
    return pl.BlockSpec(shape, lambda *_: (0,) * nd, pipeline_mode=pl.Buffered(1))


def _rms_norm(x, g):
    ms = jnp.mean(x * x, axis=-1, keepdims=True)
    return x * lax.rsqrt(ms + RMS_EPS) * g


def _sigmoid(x):
    return 1.0 / (1.0 + jnp.exp(-x))


def _dot(a, b):
    return jnp.dot(a, b, preferred_element_type=F32)


def _dot_nt(a, b):
    return lax.dot_general(a, b, (((1,), (1,)), ((), ())), preferred_element_type=F32)


def _dot_tn(a, b):
    return lax.dot_general(a, b, (((0,), (0,)), ((), ())), preferred_element_type=F32)


def _split_dot(x, w_bf16):
    hi = x.astype(BF16)
    lo = (x - hi.astype(F32)).astype(BF16)
    return _dot(hi, w_bf16) + _dot(lo, w_bf16)


A_HIST = 32
B_HIST = 8


def _even_mixer_kernel(h_ref, g_ref, win_ref, ca_ref, lng_ref, lnb_ref, cb_ref, wout_ref,
                       o_ref, ua_ref, ush_ref, ub_ref, y_ref):
    tm = h_ref.shape[1]
    i = pl.program_id(1)

    @pl.when(i == 0)
    def _():
        ua_ref[0:A_HIST, :] = jnp.zeros((A_HIST, D_A), F32)
        ub_ref[0:B_HIST, :] = jnp.zeros((B_HIST, D_B), F32)
        ua_ref[A_HIST + tm:A_HIST + tm + SUBLANES, :] = jnp.zeros((SUBLANES, D_A), F32)

    @pl.when(i > 0)
    def _():
        ua_ref[0:A_HIST, :] = ua_ref[tm:tm + A_HIST, :]
        ub_ref[0:B_HIST, :] = ub_ref[tm:tm + B_HIST, :]

    x = h_ref[0]
    hn = _rms_norm(x, g_ref[...]).astype(BF16)
    a_val = _dot(hn, win_ref[:, 0:D_A])
    a_gate = _dot(hn, win_ref[:, D_A:2 * D_A])
    ua_ref[A_HIST:A_HIST + tm, :] = a_val * _sigmoid(a_gate)
    g_c = _dot(hn, win_ref[:, 2 * D_A + D_B:2 * D_A + 2 * D_B])
    x_in = _dot(hn, win_ref[:, 2 * D_A + 2 * D_B:2 * D_A + 3 * D_B])
    ub_ref[B_HIST:B_HIST + tm, :] = g_c * x_in

    conv_b = jnp.zeros((tm, D_B), F32)
    for j in range(CONV_B_WIDTH):
        off = B_HIST - (CONV_B_WIDTH - 1) + j
        conv_b = conv_b + cb_ref[j:j + 1, :] * ub_ref[off:off + tm, :]
    g_b = _dot(hn, win_ref[:, 2 * D_A:2 * D_A + D_B])
    y_ref[:, D_A:D_A + D_B] = (g_b * conv_b).astype(BF16)

    n_sh = ush_ref.shape[1]
    for s in range(SUBLANES):
        ush_ref[s] = ua_ref[s:s + n_sh, :]

    lng = lng_ref[...]
    lnb = lnb_ref[...]
    first = A_HIST - (CONV_A_WIDTH - 1)

    def conv_block(ci, carry):
        r0 = pl.multiple_of(ci * CONV_ROWS, CONV_ROWS)
        acc = jnp.zeros((CONV_ROWS, D_A), F32)
        for j in range(CONV_A_WIDTH):
            q, s = divmod(first + j, SUBLANES)
            acc = acc + ca_ref[j:j + 1, :] * ush_ref[s, pl.ds(r0 + q * SUBLANES, CONV_ROWS), :]
        mu = jnp.mean(acc, axis=-1, keepdims=True)
        cen = acc - mu
        var = jnp.mean(cen * cen, axis=-1, keepdims=True)
        ya = cen * lax.rsqrt(var + LN_EPS) * lng + lnb
        ya = ya * _sigmoid(ya)
        y_ref[pl.ds(r0, CONV_ROWS), 0:D_A] = ya.astype(BF16)
        return carry

    lax.fori_loop(0, tm // CONV_ROWS, conv_block, 0)
    o_ref[0] = x + _dot(y_ref[...], wout_ref[...])


def _even_mixer(h, g, w_in, conv_a, ln_g, ln_b, conv_b, w_out):
    bsz, t_len, d = h.shape
    tm = ROW_TILE
    nt = t_len // tm
    n_in = w_in.shape[1]
    return pl.pallas_call(
        _even_mixer_kernel,
        out_shape=jax.ShapeDtypeStruct(h.shape, F32),
        grid=(bsz, nt),
        in_specs=[
            pl.BlockSpec((1, tm, d), lambda b, i: (b, i, 0)),
            _const_spec((1, d)),
            _const_spec((d, n_in)),
            _const_spec((CONV_A_WIDTH, D_A)),
            _const_spec((1, D_A)),
            _const_spec((1, D_A)),
            _const_spec((CONV_B_WIDTH, D_B)),
            _const_spec((D_A + D_B, d)),
        ],
        out_specs=pl.BlockSpec((1, tm, d), lambda b, i: (b, i, 0)),
        scratch_shapes=[
            pltpu.VMEM((A_HIST + tm + SUBLANES, D_A), F32),
            pltpu.VMEM((SUBLANES, A_HIST + tm, D_A), F32),
            pltpu.VMEM((B_HIST + tm, D_B), F32),
            pltpu.VMEM((tm, D_A + D_B), BF16),
        ],
        compiler_params=pltpu.CompilerParams(
            dimension_semantics=("arbitrary", "arbitrary"),
            vmem_limit_bytes=VMEM_LIMIT_BYTES),
        name="even_mixer",
    )(h, g, w_in, conv_a, ln_g, ln_b, conv_b, w_out)


F_HIST = 8


def _ffn_kernel(h_ref, g_ref, wup_ref, cw_ref, cb_ref, wdn_ref, gf_ref, o_ref,
                carry_ref, graw_ref, acc_ref, *, final_norm):
    tm = h_ref.shape[1]
    i = pl.program_id(1)

    @pl.when(i == 0)
    def _():
        carry_ref[...] = jnp.zeros(carry_ref.shape, F32)

    x = h_ref[0]
    hn = _rms_norm(x, g_ref[...]).astype(BF16)
    acc_ref[...] = x

    def chunk(c, carry):
        c0 = pl.multiple_of(c * FF_CHUNK, FF_CHUNK)
        g_raw = _dot(hn, wup_ref[:, pl.ds(c0, FF_CHUNK)])
        graw_ref[0:F_HIST, :] = carry_ref[:, pl.ds(c0, FF_CHUNK)]
        graw_ref[F_HIST:F_HIST + tm, :] = g_raw
        carry_ref[:, pl.ds(c0, FF_CHUNK)] = graw_ref[tm:tm + F_HIST, :]
        gate = cb_ref[:, pl.ds(c0, FF_CHUNK)]
        for j in range(3):
            gate = gate + cw_ref[j:j + 1, pl.ds(c0, FF_CHUNK)] * graw_ref[F_HIST - 2 + j:F_HIST - 2 + j + tm, :]
        val = _dot(hn, wup_ref[:, pl.ds(D_FF + c0, FF_CHUNK)])
        act = (gate * _sigmoid(gate) * val).astype(BF16)
        acc_ref[...] += _dot(act, wdn_ref[pl.ds(c0, FF_CHUNK), :])
        return carry

    lax.fori_loop(0, D_FF // FF_CHUNK, chunk, 0)
    out = acc_ref[...]
    if final_norm:
        out = _rms_norm(out, gf_ref[...])
    o_ref[0] = out


def _ffn(h, g, w_up, conv_w, conv_b, w_down, g_final, final_norm):
    bsz, t_len, d = h.shape
    tm = ROW_TILE
    nt = t_len // tm
    return pl.pallas_call(
        functools.partial(_ffn_kernel, final_norm=final_norm),
        out_shape=jax.ShapeDtypeStruct(h.shape, F32),
        grid=(bsz, nt),
        in_specs=[
            pl.BlockSpec((1, tm, d), lambda b, i: (b, i, 0)),
            _const_spec((1, d)),
            _const_spec((d, 2 * D_FF)),
            _const_spec((3, D_FF)),
            _const_spec((1, D_FF)),
            _const_spec((D_FF, d)),
            _const_spec((1, d)),
        ],
        out_specs=pl.BlockSpec((1, tm, d), lambda b, i: (b, i, 0)),
        scratch_shapes=[
            pltpu.VMEM((F_HIST, D_FF), F32),
            pltpu.VMEM((F_HIST + tm, FF_CHUNK), F32),
            pltpu.VMEM((tm, d), F32),
        ],
        compiler_params=pltpu.CompilerParams(
            dimension_semantics=("arbitrary", "arbitrary"),
            vmem_limit_bytes=VMEM_LIMIT_BYTES),
        name="ffn_final" if final_norm else "ffn",
    )(h, g, w_up, conv_w, conv_b, w_down, g_final)


def _odd_in_kernel(h_ref, g_ref, w_ref, qkv_ref, pr_ref):
    hn = _rms_norm(h_ref[...], g_ref[...]).astype(BF16)
    qkv_ref[...] = _dot(hn, w_ref[:, 0:ATT_COLS])
    pr_ref[...] = _dot(hn, w_ref[:, ATT_COLS:ATT_COLS + RWKV_COLS])


def _odd_in_proj(h2d, g, w_in):
    rows, d = h2d.shape
    tm = ROW_TILE
    return pl.pallas_call(
        _odd_in_kernel,
        out_shape=(jax.ShapeDtypeStruct((rows, ATT_COLS), F32),
                   jax.ShapeDtypeStruct((rows, RWKV_COLS), F32)),
        grid=(rows // tm,),
        in_specs=[
            pl.BlockSpec((tm, d), lambda i: (i, 0)),
            _const_spec((1, d)),
            _const_spec((d, ATT_COLS + RWKV_COLS)),
        ],
        out_specs=(pl.BlockSpec((tm, ATT_COLS), lambda i: (i, 0)),
                   pl.BlockSpec((tm, RWKV_COLS), lambda i: (i, 0))),
        compiler_params=pltpu.CompilerParams(
            dimension_semantics=("arbitrary",),
            vmem_limit_bytes=VMEM_LIMIT_BYTES),
        name="odd_in_proj",
    )(h2d, g, w_in)


def _rope(x, cos, sin):
    lane = lax.broadcasted_iota(jnp.int32, x.shape, 1)
    first_half = (lane % HEAD_DIM) < (HEAD_DIM // 2)
    rot = jnp.where(first_half,
                    pltpu.roll(x, LANES - HEAD_DIM // 2, axis=1),
                    pltpu.roll(x, HEAD_DIM // 2, axis=1))
    return x * cos + rot * sin


def _attn_kernel(qkv_ref, cos_ref, sin_ref, sink_ref, o_ref, q_ref, k_ref, v_ref):
    t_len = qkv_ref.shape[1]
    scale = HEAD_DIM ** -0.5
    rt = ROW_TILE

    def rope_rows(ti, carry):
        r0 = pl.multiple_of(ti * rt, 16)
        cos = cos_ref[pl.ds(r0, rt), :]
        sin = sin_ref[pl.ds(r0, rt), :]
        for gq in range(D_ATT // LANES):
            xq = qkv_ref[0, pl.ds(r0, rt), gq * LANES:(gq + 1) * LANES]
            q_ref[pl.ds(r0, rt), gq * LANES:(gq + 1) * LANES] = (_rope(xq, cos, sin) * scale).astype(BF16)
        xk = qkv_ref[0, pl.ds(r0, rt), D_ATT:D_ATT + D_KV]
        k_ref[pl.ds(r0, rt), :] = _rope(xk, cos, sin).astype(BF16)
        v_ref[pl.ds(r0, rt), :] = qkv_ref[0, pl.ds(r0, rt), D_ATT + D_KV:ATT_COLS].astype(BF16)
        return carry

    lax.fori_loop(0, t_len // rt, rope_rows, 0)

    def attend(q_start, n_q, k_start, n_band):
        t_pos = q_start + lax.broadcasted_iota(jnp.int32, (n_q, 1), 0)
        m_pos = lax.broadcasted_iota(jnp.int32, (1, N_META), 1)
        meta_ok = m_pos <= t_pos
        if n_band:
            s_pos = k_start + lax.broadcasted_iota(jnp.int32, (1, n_band), 1)
            dist = t_pos - s_pos
            band_ok = (s_pos >= N_META) & (dist >= 0) & (dist < WINDOW)
            k_band = k_ref[pl.ds(k_start, n_band), :]
            v_band = v_ref[pl.ds(k_start, n_band), :]
        k_meta = k_ref[0:N_META, :]
        v_meta = v_ref[0:N_META, :]
        for hq in range(N_Q_HEADS):
            kv = hq // GQA_GROUP
            ksl = slice(kv * HEAD_DIM, (kv + 1) * HEAD_DIM)
            qh = q_ref[pl.ds(q_start, n_q), hq * HEAD_DIM:(hq + 1) * HEAD_DIM]
            s_meta = jnp.where(meta_ok, _dot_nt(qh, k_meta[:, ksl]), NEG_INF)
            sink = sink_ref[0:1, hq:hq + 1]
            m = jnp.maximum(jnp.max(s_meta, axis=-1, keepdims=True), sink)
            if n_band:
                s_band = jnp.where(band_ok, _dot_nt(qh, k_band[:, ksl]), NEG_INF)
                m = jnp.maximum(m, jnp.max(s_band, axis=-1, keepdims=True))
            p_meta = jnp.exp(s_meta - m)
            den = jnp.sum(p_meta, axis=-1, keepdims=True) + jnp.exp(sink - m)
            acc = _dot(p_meta.astype(BF16), v_meta[:, ksl])
            if n_band:
                p_band = jnp.exp(s_band - m)
                den = den + jnp.sum(p_band, axis=-1, keepdims=True)
                acc = acc + _dot(p_band.astype(BF16), v_band[:, ksl])
            o_ref[0, pl.ds(q_start, n_q), hq * HEAD_DIM:(hq + 1) * HEAD_DIM] = (acc / den).astype(BF16)

    attend(0, N_META, 0, 0)

    def block(n, carry):
        q_start = pl.multiple_of(N_META + n * BLOCK, 16)
        k_start = pl.multiple_of(jnp.maximum(q_start - BLOCK, 0), 16)
        attend(q_start, BLOCK, k_start, 2 * BLOCK)
        return carry

    lax.fori_loop(0, (t_len - N_META) // BLOCK, block, 0)


def _attention(qkv, cos, sin, sinks):
    bsz, t_len, _ = qkv.shape
    return pl.pallas_call(
        _attn_kernel,
        out_shape=jax.ShapeDtypeStruct((bsz, t_len, D_ATT), BF16),
        grid=(bsz,),
        in_specs=[
            pl.BlockSpec((1, t_len, ATT_COLS), lambda b: (b, 0, 0)),
            _const_spec((t_len, LANES)),
            _const_spec((t_len, LANES)),
            _const_spec((1, N_Q_HEADS)),
        ],
        out_specs=pl.BlockSpec((1, t_len, D_ATT), lambda b: (b, 0, 0)),
        scratch_shapes=[
            pltpu.VMEM((t_len, D_ATT), BF16),
            pltpu.VMEM((t_len, D_KV), BF16),
            pltpu.VMEM((t_len, D_KV), BF16),
        ],
        compiler_params=pltpu.CompilerParams(
            dimension_semantics=("arbitrary",),
            vmem_limit_bytes=VMEM_LIMIT_BYTES),
        name="swa_attention",
    )(qkv, cos, sin, sinks)


def _seg_cumsum(x, rows_in_chunk, reverse):
    n = x.shape[0]
    s = 1
    while s < WKV_CHUNK:
        if reverse:
            x = x + jnp.where(rows_in_chunk < WKV_CHUNK - s, pltpu.roll(x, n - s, axis=0), 0.0)
        else:
            x = x + jnp.where(rows_in_chunk >= s, pltpu.roll(x, s, axis=0), 0.0)
        s *= 2
    return x


def _rwkv_kernel(pr_ref, mu_ref, w0_ref, a0_ref, wlora_ref, g2_ref, kk_ref, ka_ref, rk_ref,
                 lng_ref, lnb_ref, hsum_ref, mexp_ref, mbd_ref, mpair_ref,
                 o_ref,
                 prev_ref, state_ref, at_ref, rt_ref, bt_ref, kt_ref, bh_ref, kh_ref, v_ref,
                 ein_ref, rp_ref, yi_ref, gm_ref, hm_ref, y_ref, bonus_ref, gate_ref,
                 *, sub_blocks):
    tm = pr_ref.shape[1]
    i = pl.program_id(1)
    L = WKV_CHUNK

    @pl.when(i == 0)
    def _():
        prev_ref[...] = jnp.zeros(prev_ref.shape, F32)
        state_ref[...] = jnp.zeros(state_ref.shape, F32)

    x = pr_ref[0]
    row = lax.broadcasted_iota(jnp.int32, (tm, 1), 0)
    prev = jnp.where(row == 0, prev_ref[...], pltpu.roll(x, 1, axis=0))
    prev_ref[...] = x[tm - 1:tm, :]
    xm = x + (prev - x) * mu_ref[...]
    r = xm[:, 0:D_R]
    k = xm[:, D_R:2 * D_R]
    v = xm[:, 2 * D_R:3 * D_R]
    lora_in = xm[:, 3 * D_R:3 * D_R + LORA_W + LORA_A]
    lane = lax.broadcasted_iota(jnp.int32, lora_in.shape, 1)
    lora_in = jnp.where(lane < LORA_W, jnp.tanh(lora_in), lora_in).astype(BF16)
    lora = _dot(lora_in, wlora_ref[...])
    z = -(w0_ref[...] + lora[:, 0:D_R])
    softplus = jnp.maximum(z, 0.0) + jnp.log(1.0 + jnp.exp(-jnp.abs(z)))
    logw = -jnp.exp(-softplus - 0.5)
    alpha = _sigmoid(a0_ref[...] + lora[:, D_R:2 * D_R])
    gd = xm[:, 3 * D_R + LORA_W + LORA_A:RWKV_COLS]
    gate_ref[...] = _dot(_sigmoid(gd).astype(BF16), g2_ref[...])

    kk = k * kk_ref[...]
    norm = jnp.sqrt(_split_dot(kk * kk, hsum_ref[...]))
    kk = kk / jnp.maximum(norm, 1e-12)
    k2 = k * (1.0 + (alpha - 1.0) * ka_ref[...])
    bonus_ref[...] = _split_dot(r * k2 * rk_ref[...], hsum_ref[...]) * v
    a = -kk
    b = kk * alpha

    rows_in_chunk = row % L
    g_inc = _seg_cumsum(logw, rows_in_chunk, reverse=False)
    g_suf = _seg_cumsum(logw, rows_in_chunk, reverse=True) - logw
    e_in = jnp.exp(g_inc)
    e_neg = jnp.exp(-g_inc)
    e_last = jnp.exp(g_suf)
    ein_ref[...] = e_in
    at_ref[...] = (a * jnp.exp(g_inc - logw)).astype(BF16)
    rt_ref[...] = r * e_in
    bt_ref[...] = (b * e_neg).astype(BF16)
    kt_ref[...] = (k2 * e_neg).astype(BF16)
    bh_ref[...] = (b * e_last).astype(BF16)
    kh_ref[...] = (k2 * e_last).astype(BF16)
    v_ref[...] = v.astype(BF16)

    s_idx = lax.broadcasted_iota(jnp.int32, (L, LANES), 0)
    r_idx = lax.broadcasted_iota(jnp.int32, (L, LANES), 1) % L
    strict = r_idx < s_idx
    incl = r_idx <= s_idx
    eye_p = (r_idx == s_idx).astype(F32)
    d_row = lax.broadcasted_iota(jnp.int32, (LANES, LANES), 0)
    d_col = lax.broadcasted_iota(jnp.int32, (LANES, LANES), 1)
    diag = d_row == d_col
    n_rep = LANES // L

    def expand(xb):
        return jnp.concatenate([xb] * n_rep, axis=0) * mexp_ref[...]

    def pmm(xp, yp):
        ybd = jnp.concatenate([yp.astype(BF16)] * n_rep, axis=0) * mbd_ref[...]
        return _dot(xp.astype(BF16), ybd)

    def intra(c0):
        def body(cc, carry):
            r0 = pl.multiple_of((c0 + cc) * L, L)
            at = at_ref[pl.ds(r0, L), :]
            rt = rt_ref[pl.ds(r0, L), :]
            bt = bt_ref[pl.ds(r0, L), :]
            kt = kt_ref[pl.ds(r0, L), :]
            bh = bh_ref[pl.ds(r0, L), :]
            kh = kh_ref[pl.ds(r0, L), :]
            vv = v_ref[pl.ds(r0, L), :]
            w_last = ein_ref[pl.ds(r0 + L - 1, 1), :]
            lhs = jnp.concatenate([at, rt.astype(BF16)], axis=0)
            rhs = jnp.concatenate([expand(bt), expand(kt)], axis=0)
            sc = _dot_nt(lhs, rhs)
            a_ab = jnp.where(strict, sc[0:L, 0:LANES], 0.0)
            a_ak = jnp.where(strict, sc[0:L, LANES:2 * LANES], 0.0)
            m_rb = jnp.where(incl, sc[L:2 * L, 0:LANES], 0.0)
            m_rk = jnp.where(incl, sc[L:2 * L, LANES:2 * LANES], 0.0)
            p = eye_p + a_ab
            apow = a_ab
            for _ in range(3):
                apow = pmm(apow, apow)
                p = p + pmm(p, apow)
            pb = p.astype(BF16)
            ev = expand(vv)
            atp = _dot(pb, expand(at))
            uv = _dot(pb, expand(_dot(a_ak.astype(BF16), ev).astype(BF16)))
            atp_b = atp.astype(BF16)
            uv_b = uv.astype(BF16)
            rp_ref[pl.ds(r0, L), :] = rt + _dot(m_rb.astype(BF16), expand(atp_b))
            yi_ref[pl.ds(r0, L), :] = (_dot(m_rb.astype(BF16), expand(uv_b))
                                       + _dot(m_rk.astype(BF16), ev))
            for g in range(N_GROUPS):
                sl = slice(g * LANES, (g + 1) * LANES)
                gm = _dot_tn(bh[:, sl], atp_b[:, sl]) * mpair_ref[...]
                gm = gm + jnp.where(diag, w_last[:, sl], 0.0)
                hm = (_dot_tn(bh[:, sl], uv_b[:, sl]) + _dot_tn(kh[:, sl], vv[:, sl])) * mpair_ref[...]
                gm_ref[cc, g] = gm.astype(BF16)
                hm_ref[cc, g] = hm
            return carry
        return body

    def inter(c0):
        def body(cc, carry):
            r0 = pl.multiple_of((c0 + cc) * L, L)
            for g in range(N_GROUPS):
                sl = slice(g * LANES, (g + 1) * LANES)
                s2 = state_ref[g].astype(BF16)
                y_ref[pl.ds(r0, L), sl] = _dot(rp_ref[pl.ds(r0, L), sl].astype(BF16), s2) + yi_ref[pl.ds(r0, L), sl]
                state_ref[g] = _dot(gm_ref[cc, g], s2) + hm_ref[cc, g]
            return carry
        return body

    for c0, n_c in sub_blocks:
        lax.fori_loop(0, n_c, intra(c0), 0)
        lax.fori_loop(0, n_c, inter(c0), 0)

    y = y_ref[...]
    mean = _split_dot(y, hsum_ref[...]) * (1.0 / RWKV_HEAD)
    cen = y - mean
    var = _split_dot(cen * cen, hsum_ref[...]) * (1.0 / RWKV_HEAD)
    yn = cen * lax.rsqrt(var + RWKV_GN_EPS) * lng_ref[...] + lnb_ref[...]
    o_ref[0] = ((yn + bonus_ref[...]) * gate_ref[...]).astype(BF16)


def _rwkv(pr, mu, w0, a0, w_lora, g2, k_k, k_a, r_k, lnx_g, lnx_b):
    bsz, t_len, _ = pr.shape
    tm = ROW_TILE
    nt = t_len // tm
    n_chunks = tm // WKV_CHUNK
    half = (n_chunks + 1) // 2
    sub_blocks = ((0, half), (half, n_chunks - half))

    lane_head = jnp.arange(D_R) // RWKV_HEAD
    hsum = (lane_head[:, None] == lane_head[None, :]).astype(BF16)
    row_head = jnp.arange(LANES) // WKV_CHUNK
    mexp = (row_head[:, None] == lane_head[None, :]).astype(BF16)
    mbd = (row_head[:, None] == row_head[None, :]).astype(BF16)
    pair_head = jnp.arange(LANES) // RWKV_HEAD
    mpair = (pair_head[:, None] == pair_head[None, :]).astype(F32)

    tile_f32 = pltpu.VMEM((tm, D_R), F32)
    tile_bf16 = pltpu.VMEM((tm, D_R), BF16)
    return pl.pallas_call(
        functools.partial(_rwkv_kernel, sub_blocks=sub_blocks),
        out_shape=jax.ShapeDtypeStruct((bsz, t_len, D_R), BF16),
        grid=(bsz, nt),
        in_specs=[
            pl.BlockSpec((1, tm, RWKV_COLS), lambda b, i: (b, i, 0)),
            _const_spec((1, RWKV_COLS)),
            _const_spec((1, D_R)),
            _const_spec((1, D_R)),
            _const_spec((LORA_W + LORA_A, 2 * D_R)),
            _const_spec((LORA_G, D_R)),
            _const_spec((1, D_R)),
            _const_spec((1, D_R)),
            _const_spec((1, D_R)),
            _const_spec((1, D_R)),
            _const_spec((1, D_R)),
            _const_spec((D_R, D_R)),
            _const_spec((LANES, D_R)),
            _const_spec((LANES, LANES)),
            _const_spec((LANES, LANES)),
        ],
        out_specs=pl.BlockSpec((1, tm, D_R), lambda b, i: (b, i, 0)),
        scratch_shapes=[
            pltpu.VMEM((1, RWKV_COLS), F32),
            pltpu.VMEM((N_GROUPS, LANES, LANES), F32),
            tile_bf16, tile_f32, tile_bf16, tile_bf16, tile_bf16, tile_bf16, tile_bf16,
            tile_f32, tile_f32, tile_f32,
            pltpu.VMEM((half, N_GROUPS, LANES, LANES), BF16),
            pltpu.VMEM((half, N_GROUPS, LANES, LANES), F32),
            tile_f32, tile_f32, tile_f32,
        ],
        compiler_params=pltpu.CompilerParams(
            dimension_semantics=("arbitrary", "arbitrary"),
            vmem_limit_bytes=VMEM_LIMIT_BYTES),
        name="rwkv7",
    )(pr, mu, w0, a0, w_lora, g2, k_k, k_a, r_k, lnx_g, lnx_b, hsum, mexp, mbd, mpair)


def _odd_out_kernel(h_ref, ya_ref, yr_ref, w_ref, o_ref):
    o_ref[...] = (h_ref[...] + _dot(ya_ref[...], w_ref[0:D_ATT, :])
                  + _dot(yr_ref[...], w_ref[D_ATT:D_ATT + D_R, :]))


def _odd_out_proj(h2d, y_att, y_rwkv, w_out):
    rows, d = h2d.shape
    tm = ROW_TILE
    return pl.pallas_call(
        _odd_out_kernel,
        out_shape=jax.ShapeDtypeStruct(h2d.shape, F32),
        grid=(rows // tm,),
        in_specs=[
            pl.BlockSpec((tm, d), lambda i: (i, 0)),
            pl.BlockSpec((tm, D_ATT), lambda i: (i, 0)),
            pl.BlockSpec((tm, D_R), lambda i: (i, 0)),
            _const_spec((D_ATT + D_R, d)),
        ],
        out_specs=pl.BlockSpec((tm, d), lambda i: (i, 0)),
        compiler_params=pltpu.CompilerParams(
            dimension_semantics=("arbitrary",),
            vmem_limit_bytes=VMEM_LIMIT_BYTES),
        name="odd_out_proj",
    )(h2d, y_att, y_rwkv, w_out)


def _rope_tables(t_len):
    half = HEAD_DIM // 2
    inv = ROPE_THETA ** (-jnp.arange(half, dtype=F32) / half)
    ang = jnp.arange(t_len, dtype=F32)[:, None] * inv[None, :]
    cos = jnp.cos(ang)
    sin = jnp.sin(ang)
    reps = LANES // HEAD_DIM
    cos_t = jnp.tile(jnp.concatenate([cos, cos], axis=1), (1, reps))
    sin_t = jnp.tile(jnp.concatenate([-sin, sin], axis=1), (1, reps))
    return cos_t, sin_t


def kernel(x, meta_tokens, norm_mix, norm_ffn, norm_final, ev_w_in, ev_conv_a, ev_ln_a_g, ev_ln_a_b, ev_conv_b, ev_w_out, od_w_in, od_sinks, od_mu, od_w0, od_w2, od_a0, od_a2, od_g2, od_k_k, od_k_a, od_r_k, od_lnx_g, od_lnx_b, od_w_out, ff_w_up, ff_conv, ff_conv_b, ff_w_down):
    bsz, seq, d = x.shape
    depth = norm_mix.shape[0]
    t_len = N_META + seq
    meta = jnp.broadcast_to(meta_tokens[None].astype(x.dtype), (bsz, N_META, d))
    h = jnp.concatenate([meta, x], axis=1)
    row = lambda v: v.reshape(1, -1).astype(F32)
    cos_t, sin_t = _rope_tables(t_len)
    for i in range(depth):
        j = i // 2
        if i % 2 == 0:
            h = _even_mixer(h, row(norm_mix[i]), ev_w_in[j].astype(BF16), ev_conv_a[j],
                            row(ev_ln_a_g[j]), row(ev_ln_a_b[j]), ev_conv_b[j],
                            ev_w_out[j].astype(BF16))
        else:
            h2d = h.reshape(bsz * t_len, d)
            qkv, pr = _odd_in_proj(h2d, row(norm_mix[i]), od_w_in[j].astype(BF16))
            y_att = _attention(qkv.reshape(bsz, t_len, ATT_COLS), cos_t, sin_t, row(od_sinks[j]))
            zeros = jnp.zeros((LORA_W, D_R), F32)
            w_lora = jnp.concatenate(
                [jnp.concatenate([od_w2[j], zeros], axis=1),
                 jnp.concatenate([zeros, od_a2[j]], axis=1)], axis=0).astype(BF16)
            y_rwkv = _rwkv(pr.reshape(bsz, t_len, RWKV_COLS), row(od_mu[j]), row(od_w0[j]),
                           row(od_a0[j]), w_lora, od_g2[j].astype(BF16), row(od_k_k[j]),
                           row(od_k_a[j]), row(od_r_k[j]), row(od_lnx_g[j]), row(od_lnx_b[j]))
            h = _odd_out_proj(h2d, y_att.reshape(bsz * t_len, D_ATT),
                              y_rwkv.reshape(bsz * t_len, D_R),
                              od_w_out[j].astype(BF16)).reshape(bsz, t_len, d)
        last = i == depth - 1
        h = _ffn(h, row(norm_ffn[i]), ff_w_up[i].astype(BF16), ff_conv[i], row(ff_conv_b[i]),
                 ff_w_down[i].astype(BF16), row(norm_final), final_norm=last)
    return h[:, N_META:]
```

```python
import functools

import jax
import jax.numpy as jnp
from jax import lax
from jax.experimental import pallas as pl
from jax.experimental.pallas import tpu as pltpu

D_MODEL = 1024
N_META = 16
RMS_EPS = 1e-6
LN_EPS = 1e-5
D_A = 512
D_B = 512
CONV_A_WIDTH = 31
CONV_B_WIDTH = 3
HEAD_DIM = 64
N_Q_HEADS = 8
N_KV_HEADS = 2
GQA_GROUP = N_Q_HEADS // N_KV_HEADS
D_ATT = N_Q_HEADS * HEAD_DIM
D_KV = N_KV_HEADS * HEAD_DIM
WINDOW = 128
BLOCK = 128
ROPE_THETA = 10000.0
D_R = 512
N_R_HEADS = 8
RWKV_HEAD = 64
LORA_W = 64
LORA_A = 64
LORA_G = 128
RWKV_GN_EPS = 64e-5
ATT_COLS = D_ATT + 2 * D_KV
RWKV_COLS = 3 * D_R + LORA_W + LORA_A + LORA_G
D_FF = 2816
NEG_INF = -1e30

VMEM_LIMIT_BYTES = 56 * 1024 * 1024
LANES = 128
SUBLANES = 8
ROW_TILE = 688
CONV_ROWS = 16
CONV_GROUP = 4
FF_CHUNK = 256
WKV_CHUNK = 16
WKV_GROUP = 8
HEADS_PER_GROUP = LANES // RWKV_HEAD
N_GROUPS = N_R_HEADS // HEADS_PER_GROUP

F32 = jnp.float32
BF16 = jnp.bfloat16


def _const_spec(shape):
    nd = len(shape)
    return pl.BlockSpec(shape, lambda *_: (0,) * nd, pipeline_mode=pl.Buffered(1))


def _rms_norm(x, g):
    ms = jnp.mean(x * x, axis=-1, keepdims=True)
    return x * lax.rsqrt(ms + RMS_EPS) * g


def _sigmoid(x):
    return 1.0 / (1.0 + jnp.exp(-x))


def _dot(a, b):
    return jnp.dot(a, b, preferred_element_type=F32)


def _dot_nt(a, b):
    return lax.dot_general(a, b, (((1,), (1,)), ((), ())), preferred_element_type=F32)


def _dot_tn(a, b):
    return lax.dot_general(a, b, (((0,), (0,)), ((), ())), preferred_element_type=F32)


def _scaled(i, m):
    return i * m if isinstance(i, int) else pl.multiple_of(i * m, m)


def _grouped_loop(n, group, body):
    def trip(t, carry):
        body([t * group + u for u in range(group)])
        return carry

    lax.fori_loop(0, n // group, trip, 0)
    if n % group:
        body(list(range(n - n % group, n)))


def _split_dot(x, w_bf16):
    hi = x.astype(BF16)
    lo = (x - hi.astype(F32)).astype(BF16)
    return _dot(hi, w_bf16) + _dot(lo, w_bf16)


A_HIST = 32
B_HIST = 8


def _even_mixer_kernel(h_ref, g_ref, win_ref, ca_ref, lng_ref, lnb_ref, cb_ref, wout_ref,
                       o_ref, ua_ref, ush_ref, ub_ref, y_ref):
    tm = h_ref.shape[1]
    i = pl.program_id(1)

    @pl.when(i == 0)
    def _():
        ua_ref[0:A_HIST, :] = jnp.zeros((A_HIST, D_A), F32)
        ub_ref[0:B_HIST, :] = jnp.zeros((B_HIST, D_B), F32)
        ua_ref[A_HIST + tm:A_HIST + tm + SUBLANES, :] = jnp.zeros((SUBLANES, D_A), F32)

    @pl.when(i > 0)
    def _():
        ua_ref[0:A_HIST, :] = ua_ref[tm:tm + A_HIST, :]
        ub_ref[0:B_HIST, :] = ub_ref[tm:tm + B_HIST, :]

    x = h_ref[0]
    hn = _rms_norm(x, g_ref[...]).astype(BF16)
    a_val = _dot(hn, win_ref[:, 0:D_A])
    a_gate = _dot(hn, win_ref[:, D_A:2 * D_A])
    ua_ref[A_HIST:A_HIST + tm, :] = a_val * _sigmoid(a_gate)
    g_c = _dot(hn, win_ref[:, 2 * D_A + D_B:2 * D_A + 2 * D_B])
    x_in = _dot(hn, win_ref[:, 2 * D_A + 2 * D_B:2 * D_A + 3 * D_B])
    ub_ref[B_HIST:B_HIST + tm, :] = g_c * x_in

    conv_b = jnp.zeros((tm, D_B), F32)
    for j in range(CONV_B_WIDTH):
        off = B_HIST - (CONV_B_WIDTH - 1) + j
        conv_b = conv_b + cb_ref[j:j + 1, :] * ub_ref[off:off + tm, :]
    g_b = _dot(hn, win_ref[:, 2 * D_A:2 * D_A + D_B])
    y_ref[:, D_A:D_A + D_B] = (g_b * conv_b).astype(BF16)

    n_sh = ush_ref.shape[1]
    for s in range(SUBLANES):
        ush_ref[s] = ua_ref[s:s + n_sh, :]

    lng = lng_ref[...]
    lnb = lnb_ref[...]
    first = A_HIST - (CONV_A_WIDTH - 1)

    def conv_blocks(blocks):
        starts = [_scaled(ci, CONV_ROWS) for ci in blocks]
        accs = [jnp.zeros((CONV_ROWS, D_A), F32) for _ in blocks]
        for j in range(CONV_A_WIDTH):
            q, s = divmod(first + j, SUBLANES)
            w8 = ca_ref[j * SUBLANES:(j + 1) * SUBLANES, :]
            w = jnp.concatenate([w8] * (CONV_ROWS // SUBLANES), axis=0)
            for u, r0 in enumerate(starts):
                accs[u] = accs[u] + w * ush_ref[s, pl.ds(r0 + q * SUBLANES, CONV_ROWS), :]
        for acc, r0 in zip(accs, starts):
            mu = jnp.mean(acc, axis=-1, keepdims=True)
            cen = acc - mu
            var = jnp.mean(cen * cen, axis=-1, keepdims=True)
            ya = cen * lax.rsqrt(var + LN_EPS) * lng + lnb
            ya = ya * _sigmoid(ya)
            y_ref[pl.ds(r0, CONV_ROWS), 0:D_A] = ya.astype(BF16)

    _grouped_loop(tm // CONV_ROWS, CONV_GROUP, conv_blocks)
    o_ref[0] = x + _dot(y_ref[...], wout_ref[...])


def _even_mixer(h, g, w_in, conv_a, ln_g, ln_b, conv_b, w_out):
    bsz, t_len, d = h.shape
    tm = ROW_TILE
    nt = t_len // tm
    n_in = w_in.shape[1]
    return pl.pallas_call(
        _even_mixer_kernel,
        out_shape=jax.ShapeDtypeStruct(h.shape, F32),
        grid=(bsz, nt),
        in_specs=[
            pl.BlockSpec((1, tm, d), lambda b, i: (b, i, 0)),
            _const_spec((1, d)),
            _const_spec((d, n_in)),
            _const_spec((CONV_A_WIDTH * SUBLANES, D_A)),
            _const_spec((1, D_A)),
            _const_spec((1, D_A)),
            _const_spec((CONV_B_WIDTH, D_B)),
            _const_spec((D_A + D_B, d)),
        ],
        out_specs=pl.BlockSpec((1, tm, d), lambda b, i: (b, i, 0)),
        scratch_shapes=[
            pltpu.VMEM((A_HIST + tm + SUBLANES, D_A), F32),
            pltpu.VMEM((SUBLANES, A_HIST + tm, D_A), F32),
            pltpu.VMEM((B_HIST + tm, D_B), F32),
            pltpu.VMEM((tm, D_A + D_B), BF16),
        ],
        compiler_params=pltpu.CompilerParams(
            dimension_semantics=("arbitrary", "arbitrary"),
            vmem_limit_bytes=VMEM_LIMIT_BYTES),
        name="even_mixer",
    )(h, g, w_in, conv_a, ln_g, ln_b, conv_b, w_out)


F_HIST = 8


def _ffn_kernel(h_ref, g_ref, wup_ref, cw_ref, cb_ref, wdn_ref, gf_ref, o_ref,
                carry_ref, act_ref, *, final_norm):
    tm = h_ref.shape[1]
    i = pl.program_id(1)
    n_chunks = D_FF // FF_CHUNK

    @pl.when(i == 0)
    def _():
        carry_ref[...] = jnp.zeros(carry_ref.shape, F32)

    x = h_ref[0]
    hn = _rms_norm(x, g_ref[...]).astype(BF16)

    def up_proj(c):
        c0 = c * FF_CHUNK
        return (_dot(hn, wup_ref[:, c0:c0 + FF_CHUNK]),
                _dot(hn, wup_ref[:, D_FF + c0:D_FF + c0 + FF_CHUNK]))

    nxt = up_proj(0)
    for c in range(n_chunks):
        g_raw, val = nxt
        if c + 1 < n_chunks:
            nxt = up_proj(c + 1)
        cols = slice(c * FF_CHUNK, (c + 1) * FF_CHUNK)
        ext = jnp.concatenate([carry_ref[:, cols], g_raw], axis=0)
        carry_ref[:, cols] = g_raw[tm - F_HIST:tm, :]
        gate = (cb_ref[:, cols] + cw_ref[2:3, cols] * g_raw
                + cw_ref[1:2, cols] * pltpu.roll(ext, 1, axis=0)[F_HIST:, :]
                + cw_ref[0:1, cols] * pltpu.roll(ext, 2, axis=0)[F_HIST:, :])
        act_ref[:, cols] = (gate * _sigmoid(gate) * val).astype(BF16)

    out = x + _dot(act_ref[...], wdn_ref[...])
    if final_norm:
        out = _rms_norm(out, gf_ref[...])
    o_ref[0] = out


def _ffn(h, g, w_up, conv_w, conv_b, w_down, g_final, final_norm):
    bsz, t_len, d = h.shape
    tm = ROW_TILE
    nt = t_len // tm
    return pl.pallas_call(
        functools.partial(_ffn_kernel, final_norm=final_norm),
        out_shape=jax.ShapeDtypeStruct(h.shape, F32),
        grid=(bsz, nt),
        in_specs=[
            pl.BlockSpec((1, tm, d), lambda b, i: (b, i, 0)),
            _const_spec((1, d)),
            _const_spec((d, 2 * D_FF)),
            _const_spec((3, D_FF)),
            _const_spec((1, D_FF)),
            _const_spec((D_FF, d)),
            _const_spec((1, d)),
        ],
        out_specs=pl.BlockSpec((1, tm, d), lambda b, i: (b, i, 0)),
        scratch_shapes=[
            pltpu.VMEM((F_HIST, D_FF), F32),
            pltpu.VMEM((tm, D_FF), BF16),
        ],
        compiler_params=pltpu.CompilerParams(
            dimension_semantics=("arbitrary", "arbitrary"),
            vmem_limit_bytes=VMEM_LIMIT_BYTES),
        name="ffn_final" if final_norm else "ffn",
    )(h, g, w_up, conv_w, conv_b, w_down, g_final)


def _odd_in_kernel(h_ref, g_ref, w_ref, qkv_ref, pr_ref):
    hn = _rms_norm(h_ref[...], g_ref[...]).astype(BF16)
    qkv_ref[...] = _dot(hn, w_ref[:, 0:ATT_COLS])
    pr_ref[...] = _dot(hn, w_ref[:, ATT_COLS:ATT_COLS + RWKV_COLS])


def _odd_in_proj(h2d, g, w_in):
    rows, d = h2d.shape
    tm = ROW_TILE
    return pl.pallas_call(
        _odd_in_kernel,
        out_shape=(jax.ShapeDtypeStruct((rows, ATT_COLS), F32),
                   jax.ShapeDtypeStruct((rows, RWKV_COLS), F32)),
        grid=(rows // tm,),
        in_specs=[
            pl.BlockSpec((tm, d), lambda i: (i, 0)),
            _const_spec((1, d)),
            _const_spec((d, ATT_COLS + RWKV_COLS)),
        ],
        out_specs=(pl.BlockSpec((tm, ATT_COLS), lambda i: (i, 0)),
                   pl.BlockSpec((tm, RWKV_COLS), lambda i: (i, 0))),
        compiler_params=pltpu.CompilerParams(
            dimension_semantics=("arbitrary",),
            vmem_limit_bytes=VMEM_LIMIT_BYTES),
        name="odd_in_proj",
    )(h2d, g, w_in)


def _rope(x, cos, sin):
    lane = lax.broadcasted_iota(jnp.int32, x.shape, 1)
    first_half = (lane % HEAD_DIM) < (HEAD_DIM // 2)
    rot = jnp.where(first_half,
                    pltpu.roll(x, LANES - HEAD_DIM // 2, axis=1),
                    pltpu.roll(x, HEAD_DIM // 2, axis=1))
    return x * cos + rot * sin


K_PAD = BLOCK - N_META
N_EXTRA = 2 * N_META
N_KEYS = 2 * BLOCK + N_EXTRA
SINK_COL = 2 * BLOCK + N_META
Q_ROWS = GQA_GROUP * BLOCK
ATTN_GROUP = 2
V_WIDTH = 2 * LANES


def _attn_kernel(qkv_ref, cos_ref, sin_ref, bias_ref, mbias_ref, o_ref,
                 q_ref, k_ref, v_ref, kx_ref, vx_ref):
    t_len = qkv_ref.shape[1]
    scale = HEAD_DIM ** -0.5
    rt = ROW_TILE
    heads_per_group = LANES // HEAD_DIM

    k_ref[:, 0:K_PAD, :] = jnp.zeros((N_KV_HEADS, K_PAD, HEAD_DIM), BF16)
    v_ref[:, 0:K_PAD, :] = jnp.zeros((N_KV_HEADS, K_PAD, V_WIDTH), BF16)

    def rope_rows(ti, carry):
        r0 = pl.multiple_of(ti * rt, 16)
        cos = cos_ref[pl.ds(r0, rt), :]
        sin = sin_ref[pl.ds(r0, rt), :]
        for gq in range(D_ATT // LANES):
            xq = qkv_ref[0, pl.ds(r0, rt), gq * LANES:(gq + 1) * LANES]
            xq = (_rope(xq, cos, sin) * scale).astype(BF16)
            for u in range(heads_per_group):
                q_ref[gq * heads_per_group + u, pl.ds(r0, rt), :] = xq[:, u * HEAD_DIM:(u + 1) * HEAD_DIM]
        xk = _rope(qkv_ref[0, pl.ds(r0, rt), D_ATT:D_ATT + D_KV], cos, sin).astype(BF16)
        xv = qkv_ref[0, pl.ds(r0, rt), D_ATT + D_KV:ATT_COLS].astype(BF16)
        ones = jnp.ones((rt, LANES), BF16)
        for u in range(N_KV_HEADS):
            k_ref[u, pl.ds(K_PAD + r0, rt), :] = xk[:, u * HEAD_DIM:(u + 1) * HEAD_DIM]
            vu = xv[:, u * HEAD_DIM:(u + 1) * HEAD_DIM]
            v_ref[u, pl.ds(K_PAD + r0, rt), :] = jnp.concatenate([vu, vu, ones], axis=1)
        return carry

    lax.fori_loop(0, t_len // rt, rope_rows, 0)

    kx_ref[:, 0:N_META, :] = k_ref[:, K_PAD:K_PAD + N_META, :]
    kx_ref[:, N_META:N_EXTRA, :] = jnp.zeros((N_KV_HEADS, N_EXTRA - N_META, HEAD_DIM), BF16)
    vx_ref[:, 0:N_META, :] = v_ref[:, K_PAD:K_PAD + N_META, :]
    vx_ref[:, N_META:N_EXTRA, :] = jnp.concatenate(
        [jnp.zeros((N_KV_HEADS, N_EXTRA - N_META, LANES), BF16),
         jnp.ones((N_KV_HEADS, N_EXTRA - N_META, LANES), BF16)], axis=2)

    def softmax_pv(g, q_start, n_q, keys, vals, bias):
        q4 = q_ref[g * GQA_GROUP:(g + 1) * GQA_GROUP, pl.ds(q_start, n_q), :]
        q4 = q4.reshape(GQA_GROUP * n_q, HEAD_DIM)
        s = _dot_nt(q4, keys) + bias
        p = jnp.exp(s - jnp.max(s, axis=-1, keepdims=True)).astype(BF16)
        r = _dot(p, vals)
        o = r[:, 0:LANES] / r[:, LANES:V_WIDTH]
        lane = lax.broadcasted_iota(jnp.int32, (n_q, LANES), 1)
        for u in range(GQA_GROUP // heads_per_group):
            pair = jnp.where(lane < HEAD_DIM, o[2 * u * n_q:(2 * u + 1) * n_q, :],
                             o[(2 * u + 1) * n_q:(2 * u + 2) * n_q, :])
            col = (g * GQA_GROUP + 2 * u) * HEAD_DIM
            o_ref[0, pl.ds(q_start, n_q), col:col + LANES] = pair.astype(BF16)

    for g in range(N_KV_HEADS):
        softmax_pv(g, 0, N_META, kx_ref[g], vx_ref[g], mbias_ref[g])

    def blocks(ns):
        for n in ns:
            band = _scaled(n, BLOCK)
            q_start = N_META + band
            variant = min(n, 1) if isinstance(n, int) else jnp.minimum(n, 1)
            for g in range(N_KV_HEADS):
                keys = jnp.concatenate([k_ref[g, pl.ds(band, 2 * BLOCK), :], kx_ref[g]], axis=0)
                vals = jnp.concatenate([v_ref[g, pl.ds(band, 2 * BLOCK), :], vx_ref[g]], axis=0)
                softmax_pv(g, q_start, BLOCK, keys, vals, bias_ref[variant, g])

    _grouped_loop((t_len - N_META) // BLOCK, ATTN_GROUP, blocks)


def _attention(qkv, cos, sin, sinks):
    bsz, t_len, _ = qkv.shape
    def stacked_bias(n_q, visible):
        sink = sinks.reshape(N_KV_HEADS, GQA_GROUP, 1, 1).astype(F32)
        base = jnp.where(visible, 0.0, NEG_INF).astype(F32)
        col = jnp.arange(visible.shape[1])[None, :]
        tiled = jnp.broadcast_to(base[None, None], (N_KV_HEADS, GQA_GROUP) + base.shape)
        tiled = jnp.where(col == visible.shape[1] - N_EXTRA + N_META, sink, tiled)
        return tiled.reshape(N_KV_HEADS, GQA_GROUP * n_q, visible.shape[1])

    qi = jnp.arange(BLOCK)[:, None]
    kj = jnp.arange(N_KEYS)[None, :]
    in_window = (kj >= qi + 1) & (kj <= qi + WINDOW) & (kj < 2 * BLOCK)
    is_meta = (kj >= 2 * BLOCK) & (kj < 2 * BLOCK + N_META)
    bias = jnp.stack([stacked_bias(BLOCK, (in_window & (kj >= BLOCK)) | is_meta),
                      stacked_bias(BLOCK, in_window | is_meta)])
    mi = jnp.arange(N_META)[:, None]
    mj = jnp.arange(N_EXTRA)[None, :]
    mbias = stacked_bias(N_META, mj <= mi)
    return pl.pallas_call(
        _attn_kernel,
        out_shape=jax.ShapeDtypeStruct((bsz, t_len, D_ATT), BF16),
        grid=(bsz,),
        in_specs=[
            pl.BlockSpec((1, t_len, ATT_COLS), lambda b: (b, 0, 0)),
            _const_spec((t_len, LANES)),
            _const_spec((t_len, LANES)),
            _const_spec((2, N_KV_HEADS, Q_ROWS, N_KEYS)),
            _const_spec((N_KV_HEADS, GQA_GROUP * N_META, N_EXTRA)),
        ],
        out_specs=pl.BlockSpec((1, t_len, D_ATT), lambda b: (b, 0, 0)),
        scratch_shapes=[
            pltpu.VMEM((N_Q_HEADS, t_len, HEAD_DIM), BF16),
            pltpu.VMEM((N_KV_HEADS, K_PAD + t_len, HEAD_DIM), BF16),
            pltpu.VMEM((N_KV_HEADS, K_PAD + t_len, V_WIDTH), BF16),
            pltpu.VMEM((N_KV_HEADS, N_EXTRA, HEAD_DIM), BF16),
            pltpu.VMEM((N_KV_HEADS, N_EXTRA, V_WIDTH), BF16),
        ],
        compiler_params=pltpu.CompilerParams(
            dimension_semantics=("arbitrary",),
            vmem_limit_bytes=VMEM_LIMIT_BYTES),
        name="swa_attention",
    )(qkv, cos, sin, bias, mbias)


def _seg_cumsum(x, rows_in_chunk, reverse):
    n = x.shape[0]
    s = 1
    while s < WKV_CHUNK:
        if reverse:
            x = x + jnp.where(rows_in_chunk < WKV_CHUNK - s, pltpu.roll(x, n - s, axis=0), 0.0)
        else:
            x = x + jnp.where(rows_in_chunk >= s, pltpu.roll(x, s, axis=0), 0.0)
        s *= 2
    return x


def _rwkv_kernel(pr_ref, mu_ref, w0_ref, a0_ref, wlora_ref, g2_ref, kk_ref, ka_ref, rk_ref,
                 lng_ref, lnb_ref, hsum_ref, mexp_ref, mbd_ref, mpair_ref,
                 o_ref,
                 prev_ref, state_ref, at_ref, rt_ref, bt_ref, kt_ref, bh_ref, kh_ref, v_ref,
                 ein_ref, rp_ref, yi_ref, gm_ref, hm_ref, y_ref, bonus_ref, gate_ref,
                 *, sub_blocks):
    tm = pr_ref.shape[1]
    i = pl.program_id(1)
    L = WKV_CHUNK

    @pl.when(i == 0)
    def _():
        prev_ref[...] = jnp.zeros(prev_ref.shape, F32)
        state_ref[...] = jnp.zeros(state_ref.shape, F32)

    x = pr_ref[0]
    row = lax.broadcasted_iota(jnp.int32, (tm, 1), 0)
    prev = jnp.where(row == 0, prev_ref[...], pltpu.roll(x, 1, axis=0))
    prev_ref[...] = x[tm - 1:tm, :]
    xm = x + (prev - x) * mu_ref[...]
    r = xm[:, 0:D_R]
    k = xm[:, D_R:2 * D_R]
    v = xm[:, 2 * D_R:3 * D_R]
    lora_in = xm[:, 3 * D_R:3 * D_R + LORA_W + LORA_A]
    lane = lax.broadcasted_iota(jnp.int32, lora_in.shape, 1)
    lora_in = jnp.where(lane < LORA_W, jnp.tanh(lora_in), lora_in).astype(BF16)
    lora = _dot(lora_in, wlora_ref[...])
    z = -(w0_ref[...] + lora[:, 0:D_R])
    softplus = jnp.maximum(z, 0.0) + jnp.log(1.0 + jnp.exp(-jnp.abs(z)))
    logw = -jnp.exp(-softplus - 0.5)
    alpha = _sigmoid(a0_ref[...] + lora[:, D_R:2 * D_R])
    gd = xm[:, 3 * D_R + LORA_W + LORA_A:RWKV_COLS]
    gate_ref[...] = _dot(_sigmoid(gd).astype(BF16), g2_ref[...])

    kk = k * kk_ref[...]
    norm = jnp.sqrt(_split_dot(kk * kk, hsum_ref[...]))
    kk = kk / jnp.maximum(norm, 1e-12)
    k2 = k * (1.0 + (alpha - 1.0) * ka_ref[...])
    bonus_ref[...] = _split_dot(r * k2 * rk_ref[...], hsum_ref[...]) * v
    a = -kk
    b = kk * alpha

    rows_in_chunk = row % L
    g_inc = _seg_cumsum(logw, rows_in_chunk, reverse=False)
    g_suf = _seg_cumsum(logw, rows_in_chunk, reverse=True) - logw
    e_in = jnp.exp(g_inc)
    e_neg = jnp.exp(-g_inc)
    e_last = jnp.exp(g_suf)
    ein_ref[...] = e_in
    at_ref[...] = (a * jnp.exp(g_inc - logw)).astype(BF16)
    rt_ref[...] = r * e_in
    bt_ref[...] = (b * e_neg).astype(BF16)
    kt_ref[...] = (k2 * e_neg).astype(BF16)
    bh_ref[...] = (b * e_last).astype(BF16)
    kh_ref[...] = (k2 * e_last).astype(BF16)
    v_ref[...] = v.astype(BF16)

    s_idx = lax.broadcasted_iota(jnp.int32, (L, LANES), 0)
    r_idx = lax.broadcasted_iota(jnp.int32, (L, LANES), 1) % L
    strict = r_idx < s_idx
    incl = r_idx <= s_idx
    eye_p = (r_idx == s_idx).astype(F32)
    d_row = lax.broadcasted_iota(jnp.int32, (LANES, LANES), 0)
    d_col = lax.broadcasted_iota(jnp.int32, (LANES, LANES), 1)
    diag = d_row == d_col
    n_rep = LANES // L

    def expand(xb):
        return jnp.concatenate([xb] * n_rep, axis=0) * mexp_ref[...]

    def pmm(xp, yp):
        ybd = jnp.concatenate([yp.astype(BF16)] * n_rep, axis=0) * mbd_ref[...]
        return _dot(xp.astype(BF16), ybd)

    def intra(c0):
        def body(ccs):
            n = range(len(ccs))
            r0 = [_scaled(c0 + cc, L) for cc in ccs]
            at = [at_ref[pl.ds(r, L), :] for r in r0]
            rt = [rt_ref[pl.ds(r, L), :] for r in r0]
            vv = [v_ref[pl.ds(r, L), :] for r in r0]
            sc = [_dot_nt(jnp.concatenate([at[i], rt[i].astype(BF16)], axis=0),
                          jnp.concatenate([expand(bt_ref[pl.ds(r0[i], L), :]),
                                           expand(kt_ref[pl.ds(r0[i], L), :])], axis=0)) for i in n]
            a_ab = [jnp.where(strict, s[0:L, 0:LANES], 0.0) for s in sc]
            a_ak = [jnp.where(strict, s[0:L, LANES:2 * LANES], 0.0).astype(BF16) for s in sc]
            m_rb = [jnp.where(incl, s[L:2 * L, 0:LANES], 0.0).astype(BF16) for s in sc]
            m_rk = [jnp.where(incl, s[L:2 * L, LANES:2 * LANES], 0.0).astype(BF16) for s in sc]
            ev = [expand(v) for v in vv]
            av = [_dot(a_ak[i], ev[i]).astype(BF16) for i in n]
            p = [eye_p + a for a in a_ab]
            apow = a_ab
            for _ in range(3):
                apow = [pmm(a, a) for a in apow]
                p = [p[i] + pmm(p[i], apow[i]) for i in n]
            pb = [x.astype(BF16) for x in p]
            atp = [_dot(pb[i], expand(at[i])).astype(BF16) for i in n]
            uv = [_dot(pb[i], expand(av[i])).astype(BF16) for i in n]
            for i in n:
                rp_ref[pl.ds(r0[i], L), :] = rt[i] + _dot(m_rb[i], expand(atp[i]))
            for i in n:
                yi_ref[pl.ds(r0[i], L), :] = _dot(m_rb[i], expand(uv[i])) + _dot(m_rk[i], ev[i])
            for i in n:
                bh = bh_ref[pl.ds(r0[i], L), :]
                kh = kh_ref[pl.ds(r0[i], L), :]
                w_last = ein_ref[pl.ds(r0[i] + L - 1, 1), :]
                for g in range(N_GROUPS):
                    sl = slice(g * LANES, (g + 1) * LANES)
                    gm = _dot_tn(bh[:, sl], atp[i][:, sl]) * mpair_ref[...]
                    gm = gm + jnp.where(diag, w_last[:, sl], 0.0)
                    hm = (_dot_tn(bh[:, sl], uv[i][:, sl])
                          + _dot_tn(kh[:, sl], vv[i][:, sl])) * mpair_ref[...]
                    gm_ref[ccs[i], g] = gm.astype(BF16)
                    hm_ref[ccs[i], g] = hm
        return body

    def inter(c0):
        def body(cc, carry):
            r0 = pl.multiple_of((c0 + cc) * L, L)
            for g in range(N_GROUPS):
                sl = slice(g * LANES, (g + 1) * LANES)
                s2 = state_ref[g].astype(BF16)
                y_ref[pl.ds(r0, L), sl] = _dot(rp_ref[pl.ds(r0, L), sl].astype(BF16), s2) + yi_ref[pl.ds(r0, L), sl]
                state_ref[g] = _dot(gm_ref[cc, g], s2) + hm_ref[cc, g]
            return carry
        return body

    for c0, n_c in sub_blocks:
        _grouped_loop(n_c, WKV_GROUP, intra(c0))
        lax.fori_loop(0, n_c, inter(c0), 0)

    y = y_ref[...]
    mean = _split_dot(y, hsum_ref[...]) * (1.0 / RWKV_HEAD)
    cen = y - mean
    var = _split_dot(cen * cen, hsum_ref[...]) * (1.0 / RWKV_HEAD)
    yn = cen * lax.rsqrt(var + RWKV_GN_EPS) * lng_ref[...] + lnb_ref[...]
    o_ref[0] = ((yn + bonus_ref[...]) * gate_ref[...]).astype(BF16)


def _rwkv(pr, mu, w0, a0, w_lora, g2, k_k, k_a, r_k, lnx_g, lnx_b):
    bsz, t_len, _ = pr.shape
    tm = ROW_TILE
    nt = t_len // tm
    n_chunks = tm // WKV_CHUNK
    half = -(-n_chunks // (2 * WKV_GROUP)) * WKV_GROUP
    sub_blocks = ((0, half), (half, n_chunks - half))

    lane_head = jnp.arange(D_R) // RWKV_HEAD
    hsum = (lane_head[:, None] == lane_head[None, :]).astype(BF16)
    row_head = jnp.arange(LANES) // WKV_CHUNK
    mexp = (row_head[:, None] == lane_head[None, :]).astype(BF16)
    mbd = (row_head[:, None] == row_head[None, :]).astype(BF16)
    pair_head = jnp.arange(LANES) // RWKV_HEAD
    mpair = (pair_head[:, None] == pair_head[None, :]).astype(F32)

    tile_f32 = pltpu.VMEM((tm, D_R), F32)
    tile_bf16 = pltpu.VMEM((tm, D_R), BF16)
    return pl.pallas_call(
        functools.partial(_rwkv_kernel, sub_blocks=sub_blocks),
        out_shape=jax.ShapeDtypeStruct((bsz, t_len, D_R), BF16),
        grid=(bsz, nt),
        in_specs=[
            pl.BlockSpec((1, tm, RWKV_COLS), lambda b, i: (b, i, 0)),
            _const_spec((1, RWKV_COLS)),
            _const_spec((1, D_R)),
            _const_spec((1, D_R)),
            _const_spec((LORA_W + LORA_A, 2 * D_R)),
            _const_spec((LORA_G, D_R)),
            _const_spec((1, D_R)),
            _const_spec((1, D_R)),
            _const_spec((1, D_R)),
            _const_spec((1, D_R)),
            _const_spec((1, D_R)),
            _const_spec((D_R, D_R)),
            _const_spec((LANES, D_R)),
            _const_spec((LANES, LANES)),
            _const_spec((LANES, LANES)),
        ],
        out_specs=pl.BlockSpec((1, tm, D_R), lambda b, i: (b, i, 0)),
        scratch_shapes=[
            pltpu.VMEM((1, RWKV_COLS), F32),
            pltpu.VMEM((N_GROUPS, LANES, LANES), F32),
            tile_bf16, tile_f32, tile_bf16, tile_bf16, tile_bf16, tile_bf16, tile_bf16,
            tile_f32, tile_f32, tile_f32,
            pltpu.VMEM((half, N_GROUPS, LANES, LANES), BF16),
            pltpu.VMEM((half, N_GROUPS, LANES, LANES), F32),
            tile_f32, tile_f32, tile_f32,
        ],
        compiler_params=pltpu.CompilerParams(
            dimension_semantics=("arbitrary", "arbitrary"),
            vmem_limit_bytes=VMEM_LIMIT_BYTES),
        name="rwkv7",
    )(pr, mu, w0, a0, w_lora, g2, k_k, k_a, r_k, lnx_g, lnx_b, hsum, mexp, mbd, mpair)


def _odd_out_kernel(h_ref, ya_ref, yr_ref, w_ref, o_ref):
    o_ref[...] = (h_ref[...] + _dot(ya_ref[...], w_ref[0:D_ATT, :])
                  + _dot(yr_ref[...], w_ref[D_ATT:D_ATT + D_R, :]))


def _odd_out_proj(h2d, y_att, y_rwkv, w_out):
    rows, d = h2d.shape
    tm = ROW_TILE
    return pl.pallas_call(
        _odd_out_kernel,
        out_shape=jax.ShapeDtypeStruct(h2d.shape, F32),
        grid=(rows // tm,),
        in_specs=[
            pl.BlockSpec((tm, d), lambda i: (i, 0)),
            pl.BlockSpec((tm, D_ATT), lambda i: (i, 0)),
            pl.BlockSpec((tm, D_R), lambda i: (i, 0)),
            _const_spec((D_ATT + D_R, d)),
        ],
        out_specs=pl.BlockSpec((tm, d), lambda i: (i, 0)),
        compiler_params=pltpu.CompilerParams(
            dimension_semantics=("arbitrary",),
            vmem_limit_bytes=VMEM_LIMIT_BYTES),
        name="odd_out_proj",
    )(h2d, y_att, y_rwkv, w_out)


def _rope_tables(t_len):
    half = HEAD_DIM // 2
    inv = ROPE_THETA ** (-jnp.arange(half, dtype=F32) / half)
    ang = jnp.arange(t_len, dtype=F32)[:, None] * inv[None, :]
    cos = jnp.cos(ang)
    sin = jnp.sin(ang)
    reps = LANES // HEAD_DIM
    cos_t = jnp.tile(jnp.concatenate([cos, cos], axis=1), (1, reps))
    sin_t = jnp.tile(jnp.concatenate([-sin, sin], axis=1), (1, reps))
    return cos_t, sin_t


def kernel(x, meta_tokens, norm_mix, norm_ffn, norm_final, ev_w_in, ev_conv_a, ev_ln_a_g, ev_ln_a_b, ev_conv_b, ev_w_out, od_w_in, od_sinks, od_mu, od_w0, od_w2, od_a0, od_a2, od_g2, od_k_k, od_k_a, od_r_k, od_lnx_g, od_lnx_b, od_w_out, ff_w_up, ff_conv, ff_conv_b, ff_w_down):
    bsz, seq, d = x.shape
    depth = norm_mix.shape[0]
    t_len = N_META + seq
    meta = jnp.broadcast_to(meta_tokens[None].astype(x.dtype), (bsz, N_META, d))
    h = jnp.concatenate([meta, x], axis=1)
    row = lambda v: v.reshape(1, -1).astype(F32)
    cos_t, sin_t = _rope_tables(t_len)
    for i in range(depth):
        j = i // 2
        if i % 2 == 0:
            h = _even_mixer(h, row(norm_mix[i]), ev_w_in[j].astype(BF16),
                            jnp.repeat(ev_conv_a[j], SUBLANES, axis=0),
                            row(ev_ln_a_g[j]), row(ev_ln_a_b[j]), ev_conv_b[j],
                            ev_w_out[j].astype(BF16))
        else:
            h2d = h.reshape(bsz * t_len, d)
            qkv, pr = _odd_in_proj(h2d, row(norm_mix[i]), od_w_in[j].astype(BF16))
            y_att = _attention(qkv.reshape(bsz, t_len, ATT_COLS), cos_t, sin_t, row(od_sinks[j]))
            zeros = jnp.zeros((LORA_W, D_R), F32)
            w_lora = jnp.concatenate(
                [jnp.concatenate([od_w2[j], zeros], axis=1),
                 jnp.concatenate([zeros, od_a2[j]], axis=1)], axis=0).astype(BF16)
            y_rwkv = _rwkv(pr.reshape(bsz, t_len, RWKV_COLS), row(od_mu[j]), row(od_w0[j]),
                           row(od_a0[j]), w_lora, od_g2[j].astype(BF16), row(od_k_k[j]),
                           row(od_k_a[j]), row(od_r_k[j]), row(od_lnx_g[j]), row(od_lnx_b[j]))
            h = _odd_out_proj(h2d, y_att.reshape(bsz * t_len, D_ATT),
                              y_rwkv.reshape(bsz * t_len, D_R),
                              od_w_out[j].astype(BF16)).reshape(bsz, t_len, d)
        last = i == depth - 1
        h = _ffn(h, row(norm_ffn[i]), ff_w_up[i].astype(BF16), ff_conv[i], row(ff_conv_b[i]),
                 ff_w_down[i].astype(BF16), row(norm_final), final_norm=last)
    return h[:, N_META:]
```

```python
import functools

import jax
import jax.numpy as jnp
from jax import lax
from jax.experimental import pallas as pl
from jax.experimental.pallas import tpu as pltpu

D_MODEL = 1024
N_META = 16
RMS_EPS = 1e-6
LN_EPS = 1e-5
D_A = 512
D_B = 512
CONV_A_WIDTH = 31
CONV_B_WIDTH = 3
HEAD_DIM = 64
N_Q_HEADS = 8
N_KV_HEADS = 2
GQA_GROUP = N_Q_HEADS // N_KV_HEADS
D_ATT = N_Q_HEADS * HEAD_DIM
D_KV = N_KV_HEADS * HEAD_DIM
WINDOW = 128
BLOCK = 128
ROPE_THETA = 10000.0
D_R = 512
N_R_HEADS = 8
RWKV_HEAD = 64
LORA_W = 64
LORA_A = 64
LORA_G = 128
RWKV_GN_EPS = 64e-5
ATT_COLS = D_ATT + 2 * D_KV
RWKV_COLS = 3 * D_R + LORA_W + LORA_A + LORA_G
D_FF = 2816
NEG_INF = -1e30

VMEM_LIMIT_BYTES = 56 * 1024 * 1024
LANES = 128
SUBLANES = 8
ROW_TILE = 688
CONV_ROWS = 16
CONV_GROUP = 4
FF_CHUNK = 256
WKV_CHUNK = 16
WKV_GROUP = 8
HEADS_PER_GROUP = LANES // RWKV_HEAD
N_GROUPS = N_R_HEADS // HEADS_PER_GROUP

F32 = jnp.float32
BF16 = jnp.bfloat16


def _const_spec(shape):
    nd = len(shape)
    return pl.BlockSpec(shape, lambda *_: (0,) * nd, pipeline_mode=pl.Buffered(1))


def _rms_norm(x, g):
    ms = jnp.mean(x * x, axis=-1, keepdims=True)
    return x * lax.rsqrt(ms + RMS_EPS) * g


def _sigmoid(x):
    return 1.0 / (1.0 + jnp.exp(-x))


def _dot(a, b):
    return jnp.dot(a, b, preferred_element_type=F32)


def _dot_nt(a, b):
    return lax.dot_general(a, b, (((1,), (1,)), ((), ())), preferred_element_type=F32)


def _dot_tn(a, b):
    return lax.dot_general(a, b, (((0,), (0,)), ((), ())), preferred_element_type=F32)


def _scaled(i, m):
    return i * m if isinstance(i, int) else pl.multiple_of(i * m, m)


def _grouped_loop(n, group, body):
    def trip(t, carry):
        body([t * group + u for u in range(group)])
        return carry

    lax.fori_loop(0, n // group, trip, 0)
    if n % group:
        body(list(range(n - n % group, n)))


def _split_dot(x, w_bf16):
    hi = x.astype(BF16)
    lo = (x - hi.astype(F32)).astype(BF16)
    return _dot(hi, w_bf16) + _dot(lo, w_bf16)


def _split_dot_lhs(w_bf16, x):
    hi = x.astype(BF16)
    lo = (x - hi.astype(F32)).astype(BF16)
    return _dot(w_bf16, hi) + _dot(w_bf16, lo)


A_HIST = 32
B_HIST = 8


def _even_mixer_kernel(h_ref, g_ref, win_ref, ca_ref, lng_ref, lnb_ref, cb_ref, wout_ref,
                       o_ref, ua_ref, ush_ref, ub_ref, y_ref):
    tm = h_ref.shape[1]
    i = pl.program_id(1)

    @pl.when(i == 0)
    def _():
        ua_ref[0:A_HIST, :] = jnp.zeros((A_HIST, D_A), F32)
        ub_ref[0:B_HIST, :] = jnp.zeros((B_HIST, D_B), F32)
        ua_ref[A_HIST + tm:A_HIST + tm + SUBLANES, :] = jnp.zeros((SUBLANES, D_A), F32)

    @pl.when(i > 0)
    def _():
        ua_ref[0:A_HIST, :] = ua_ref[tm:tm + A_HIST, :]
        ub_ref[0:B_HIST, :] = ub_ref[tm:tm + B_HIST, :]

    x = h_ref[0]
    hn = _rms_norm(x, g_ref[...]).astype(BF16)
    a_val = _dot(hn, win_ref[:, 0:D_A])
    a_gate = _dot(hn, win_ref[:, D_A:2 * D_A])
    ua_ref[A_HIST:A_HIST + tm, :] = a_val * _sigmoid(a_gate)
    g_c = _dot(hn, win_ref[:, 2 * D_A + D_B:2 * D_A + 2 * D_B])
    x_in = _dot(hn, win_ref[:, 2 * D_A + 2 * D_B:2 * D_A + 3 * D_B])
    ub_ref[B_HIST:B_HIST + tm, :] = g_c * x_in

    conv_b = jnp.zeros((tm, D_B), F32)
    for j in range(CONV_B_WIDTH):
        off = B_HIST - (CONV_B_WIDTH - 1) + j
        conv_b = conv_b + cb_ref[j:j + 1, :] * ub_ref[off:off + tm, :]
    g_b = _dot(hn, win_ref[:, 2 * D_A:2 * D_A + D_B])
    y_ref[:, D_A:D_A + D_B] = (g_b * conv_b).astype(BF16)

    n_sh = ush_ref.shape[1]
    for s in range(SUBLANES):
        ush_ref[s] = ua_ref[s:s + n_sh, :]

    lng = lng_ref[...]
    lnb = lnb_ref[...]
    first = A_HIST - (CONV_A_WIDTH - 1)

    def conv_blocks(blocks):
        starts = [_scaled(ci, CONV_ROWS) for ci in blocks]
        accs = [jnp.zeros((CONV_ROWS, D_A), F32) for _ in blocks]
        for j in range(CONV_A_WIDTH):
            q, s = divmod(first + j, SUBLANES)
            w8 = ca_ref[j * SUBLANES:(j + 1) * SUBLANES, :]
            w = jnp.concatenate([w8] * (CONV_ROWS // SUBLANES), axis=0)
            for u, r0 in enumerate(starts):
                accs[u] = accs[u] + w * ush_ref[s, pl.ds(r0 + q * SUBLANES, CONV_ROWS), :]
        for acc, r0 in zip(accs, starts):
            mu = jnp.mean(acc, axis=-1, keepdims=True)
            cen = acc - mu
            var = jnp.mean(cen * cen, axis=-1, keepdims=True)
            ya = cen * lax.rsqrt(var + LN_EPS) * lng + lnb
            ya = ya * _sigmoid(ya)
            y_ref[pl.ds(r0, CONV_ROWS), 0:D_A] = ya.astype(BF16)

    n_blocks = tm // CONV_ROWS
    for b0 in range(0, n_blocks, CONV_GROUP):
        conv_blocks(list(range(b0, min(b0 + CONV_GROUP, n_blocks))))
    o_ref[0] = x + _dot(y_ref[...], wout_ref[...])


def _even_mixer(h, g, w_in, conv_a, ln_g, ln_b, conv_b, w_out):
    bsz, t_len, d = h.shape
    tm = ROW_TILE
    nt = t_len // tm
    n_in = w_in.shape[1]
    return pl.pallas_call(
        _even_mixer_kernel,
        out_shape=jax.ShapeDtypeStruct(h.shape, F32),
        grid=(bsz, nt),
        in_specs=[
            pl.BlockSpec((1, tm, d), lambda b, i: (b, i, 0)),
            _const_spec((1, d)),
            _const_spec((d, n_in)),
            _const_spec((CONV_A_WIDTH * SUBLANES, D_A)),
            _const_spec((1, D_A)),
            _const_spec((1, D_A)),
            _const_spec((CONV_B_WIDTH, D_B)),
            _const_spec((D_A + D_B, d)),
        ],
        out_specs=pl.BlockSpec((1, tm, d), lambda b, i: (b, i, 0)),
        scratch_shapes=[
            pltpu.VMEM((A_HIST + tm + SUBLANES, D_A), F32),
            pltpu.VMEM((SUBLANES, A_HIST + tm, D_A), F32),
            pltpu.VMEM((B_HIST + tm, D_B), F32),
            pltpu.VMEM((tm, D_A + D_B), BF16),
        ],
        compiler_params=pltpu.CompilerParams(
            dimension_semantics=("arbitrary", "arbitrary"),
            vmem_limit_bytes=VMEM_LIMIT_BYTES),
        name="even_mixer",
    )(h, g, w_in, conv_a, ln_g, ln_b, conv_b, w_out)


F_HIST = 8


def _ffn_kernel(*refs, final_norm, mixer_out):
    if mixer_out:
        h_ref, ya_ref, yr_ref, wo_ref = refs[:4]
        refs = refs[3:]
    else:
        h_ref = refs[0]
    _, g_ref, wup_ref, cw_ref, cb_ref, wdn_ref, gf_ref, o_ref, carry_ref, act_ref = refs
    tm = h_ref.shape[1]
    i = pl.program_id(1)
    n_chunks = D_FF // FF_CHUNK

    @pl.when(i == 0)
    def _():
        carry_ref[...] = jnp.zeros(carry_ref.shape, F32)

    x = h_ref[0]
    if mixer_out:
        x = (x + _dot(ya_ref[0], wo_ref[0:D_ATT, :]) + _dot(yr_ref[0], wo_ref[D_ATT:D_ATT + D_R, :]))
    hn = _rms_norm(x, g_ref[...]).astype(BF16)

    def up_proj(c):
        c0 = c * FF_CHUNK
        return (_dot(hn, wup_ref[:, c0:c0 + FF_CHUNK]),
                _dot(hn, wup_ref[:, D_FF + c0:D_FF + c0 + FF_CHUNK]))

    nxt = up_proj(0)
    for c in range(n_chunks):
        g_raw, val = nxt
        if c + 1 < n_chunks:
            nxt = up_proj(c + 1)
        cols = slice(c * FF_CHUNK, (c + 1) * FF_CHUNK)
        ext = jnp.concatenate([carry_ref[:, cols], g_raw], axis=0)
        carry_ref[:, cols] = g_raw[tm - F_HIST:tm, :]
        gate = (cb_ref[:, cols] + cw_ref[2:3, cols] * g_raw
                + cw_ref[1:2, cols] * pltpu.roll(ext, 1, axis=0)[F_HIST:, :]
                + cw_ref[0:1, cols] * pltpu.roll(ext, 2, axis=0)[F_HIST:, :])
        act_ref[:, cols] = (gate * _sigmoid(gate) * val).astype(BF16)

    out = x + _dot(act_ref[...], wdn_ref[...])
    if final_norm:
        out = _rms_norm(out, gf_ref[...])
    o_ref[0] = out


def _ffn(h, g, w_up, conv_w, conv_b, w_down, g_final, final_norm, mixer_out=None):
    bsz, t_len, d = h.shape
    tm = ROW_TILE
    nt = t_len // tm
    mix_args, mix_specs = (), []
    if mixer_out is not None:
        y_att, y_rwkv, w_out = mixer_out
        mix_args = (y_att, y_rwkv, w_out)
        mix_specs = [pl.BlockSpec((1, tm, D_ATT), lambda b, i: (b, i, 0)),
                     pl.BlockSpec((1, tm, D_R), lambda b, i: (b, i, 0)),
                     _const_spec((D_ATT + D_R, d))]
    return pl.pallas_call(
        functools.partial(_ffn_kernel, final_norm=final_norm, mixer_out=mixer_out is not None),
        out_shape=jax.ShapeDtypeStruct(h.shape, F32),
        grid=(bsz, nt),
        in_specs=[
            pl.BlockSpec((1, tm, d), lambda b, i: (b, i, 0)),
            *mix_specs,
            _const_spec((1, d)),
            _const_spec((d, 2 * D_FF)),
            _const_spec((3, D_FF)),
            _const_spec((1, D_FF)),
            _const_spec((D_FF, d)),
            _const_spec((1, d)),
        ],
        out_specs=pl.BlockSpec((1, tm, d), lambda b, i: (b, i, 0)),
        scratch_shapes=[
            pltpu.VMEM((F_HIST, D_FF), F32),
            pltpu.VMEM((tm, D_FF), BF16),
        ],
        compiler_params=pltpu.CompilerParams(
            dimension_semantics=("arbitrary", "arbitrary"),
            vmem_limit_bytes=VMEM_LIMIT_BYTES),
        name="ffn_final" if final_norm else "ffn",
    )(h, *mix_args, g, w_up, conv_w, conv_b, w_down, g_final)


def _odd_in_kernel(h_ref, g_ref, w_ref, qkv_ref, pr_ref):
    hn = _rms_norm(h_ref[...], g_ref[...]).astype(BF16)
    qkv_ref[...] = _dot(hn, w_ref[:, 0:ATT_COLS])
    pr_ref[...] = _dot(hn, w_ref[:, ATT_COLS:ATT_COLS + RWKV_COLS])


def _odd_in_proj(h2d, g, w_in):
    rows, d = h2d.shape
    tm = ROW_TILE
    return pl.pallas_call(
        _odd_in_kernel,
        out_shape=(jax.ShapeDtypeStruct((rows, ATT_COLS), F32),
                   jax.ShapeDtypeStruct((rows, RWKV_COLS), F32)),
        grid=(rows // tm,),
        in_specs=[
            pl.BlockSpec((tm, d), lambda i: (i, 0)),
            _const_spec((1, d)),
            _const_spec((d, ATT_COLS + RWKV_COLS)),
        ],
        out_specs=(pl.BlockSpec((tm, ATT_COLS), lambda i: (i, 0)),
                   pl.BlockSpec((tm, RWKV_COLS), lambda i: (i, 0))),
        compiler_params=pltpu.CompilerParams(
            dimension_semantics=("arbitrary",),
            vmem_limit_bytes=VMEM_LIMIT_BYTES),
        name="odd_in_proj",
    )(h2d, g, w_in)


def _rope(x, cos, sin):
    lane = lax.broadcasted_iota(jnp.int32, x.shape, 1)
    first_half = (lane % HEAD_DIM) < (HEAD_DIM // 2)
    rot = jnp.where(first_half,
                    pltpu.roll(x, LANES - HEAD_DIM // 2, axis=1),
                    pltpu.roll(x, HEAD_DIM // 2, axis=1))
    return x * cos + rot * sin


K_PAD = BLOCK - N_META
N_EXTRA = 2 * N_META
N_KEYS = 2 * BLOCK + N_EXTRA
SINK_COL = 2 * BLOCK + N_META
Q_ROWS = GQA_GROUP * BLOCK
ATTN_GROUP = 2
V_WIDTH = 2 * LANES


def _attn_kernel(qkv_ref, cos_ref, sin_ref, bias_ref, mbias_ref, o_ref,
                 q_ref, k_ref, v_ref, kx_ref, vx_ref):
    t_len = qkv_ref.shape[1]
    scale = HEAD_DIM ** -0.5
    rt = ROW_TILE
    heads_per_group = LANES // HEAD_DIM

    k_ref[:, 0:K_PAD, :] = jnp.zeros((N_KV_HEADS, K_PAD, HEAD_DIM), BF16)
    v_ref[:, 0:K_PAD, :] = jnp.zeros((N_KV_HEADS, K_PAD, V_WIDTH), BF16)

    def rope_rows(ti, carry):
        r0 = pl.multiple_of(ti * rt, 16)
        cos = cos_ref[pl.ds(r0, rt), :]
        sin = sin_ref[pl.ds(r0, rt), :]
        for gq in range(D_ATT // LANES):
            xq = qkv_ref[0, pl.ds(r0, rt), gq * LANES:(gq + 1) * LANES]
            xq = (_rope(xq, cos, sin) * scale).astype(BF16)
            for u in range(heads_per_group):
                q_ref[gq * heads_per_group + u, pl.ds(r0, rt), :] = xq[:, u * HEAD_DIM:(u + 1) * HEAD_DIM]
        xk = _rope(qkv_ref[0, pl.ds(r0, rt), D_ATT:D_ATT + D_KV], cos, sin).astype(BF16)
        xv = qkv_ref[0, pl.ds(r0, rt), D_ATT + D_KV:ATT_COLS].astype(BF16)
        ones = jnp.ones((rt, LANES), BF16)
        for u in range(N_KV_HEADS):
            k_ref[u, pl.ds(K_PAD + r0, rt), :] = xk[:, u * HEAD_DIM:(u + 1) * HEAD_DIM]
            vu = xv[:, u * HEAD_DIM:(u + 1) * HEAD_DIM]
            v_ref[u, pl.ds(K_PAD + r0, rt), :] = jnp.concatenate([vu, vu, ones], axis=1)
        return carry

    lax.fori_loop(0, t_len // rt, rope_rows, 0)

    kx_ref[:, 0:N_META, :] = k_ref[:, K_PAD:K_PAD + N_META, :]
    kx_ref[:, N_META:N_EXTRA, :] = jnp.zeros((N_KV_HEADS, N_EXTRA - N_META, HEAD_DIM), BF16)
    vx_ref[:, 0:N_META, :] = v_ref[:, K_PAD:K_PAD + N_META, :]
    vx_ref[:, N_META:N_EXTRA, :] = jnp.concatenate(
        [jnp.zeros((N_KV_HEADS, N_EXTRA - N_META, LANES), BF16),
         jnp.ones((N_KV_HEADS, N_EXTRA - N_META, LANES), BF16)], axis=2)

    def softmax_pv(g, q_start, n_q, keys, vals, bias):
        q4 = q_ref[g * GQA_GROUP:(g + 1) * GQA_GROUP, pl.ds(q_start, n_q), :]
        q4 = q4.reshape(GQA_GROUP * n_q, HEAD_DIM)
        s = _dot_nt(q4, keys) + bias
        p = jnp.exp(s - jnp.max(s, axis=-1, keepdims=True)).astype(BF16)
        r = _dot(p, vals)
        o = r[:, 0:LANES] / r[:, LANES:V_WIDTH]
        lane = lax.broadcasted_iota(jnp.int32, (n_q, LANES), 1)
        for u in range(GQA_GROUP // heads_per_group):
            pair = jnp.where(lane < HEAD_DIM, o[2 * u * n_q:(2 * u + 1) * n_q, :],
                             o[(2 * u + 1) * n_q:(2 * u + 2) * n_q, :])
            col = (g * GQA_GROUP + 2 * u) * HEAD_DIM
            o_ref[0, pl.ds(q_start, n_q), col:col + LANES] = pair.astype(BF16)

    for g in range(N_KV_HEADS):
        softmax_pv(g, 0, N_META, kx_ref[g], vx_ref[g], mbias_ref[g])

    def blocks(ns):
        for n in ns:
            band = _scaled(n, BLOCK)
            q_start = N_META + band
            variant = min(n, 1) if isinstance(n, int) else jnp.minimum(n, 1)
            for g in range(N_KV_HEADS):
                keys = jnp.concatenate([k_ref[g, pl.ds(band, 2 * BLOCK), :], kx_ref[g]], axis=0)
                vals = jnp.concatenate([v_ref[g, pl.ds(band, 2 * BLOCK), :], vx_ref[g]], axis=0)
                softmax_pv(g, q_start, BLOCK, keys, vals, bias_ref[variant, g])

    _grouped_loop((t_len - N_META) // BLOCK, ATTN_GROUP, blocks)


def _attention(qkv, cos, sin, sinks):
    bsz, t_len, _ = qkv.shape
    def stacked_bias(n_q, visible):
        sink = sinks.reshape(N_KV_HEADS, GQA_GROUP, 1, 1).astype(F32)
        base = jnp.where(visible, 0.0, NEG_INF).astype(F32)
        col = jnp.arange(visible.shape[1])[None, :]
        tiled = jnp.broadcast_to(base[None, None], (N_KV_HEADS, GQA_GROUP) + base.shape)
        tiled = jnp.where(col == visible.shape[1] - N_EXTRA + N_META, sink, tiled)
        return tiled.reshape(N_KV_HEADS, GQA_GROUP * n_q, visible.shape[1])

    qi = jnp.arange(BLOCK)[:, None]
    kj = jnp.arange(N_KEYS)[None, :]
    in_window = (kj >= qi + 1) & (kj <= qi + WINDOW) & (kj < 2 * BLOCK)
    is_meta = (kj >= 2 * BLOCK) & (kj < 2 * BLOCK + N_META)
    bias = jnp.stack([stacked_bias(BLOCK, (in_window & (kj >= BLOCK)) | is_meta),
                      stacked_bias(BLOCK, in_window | is_meta)])
    mi = jnp.arange(N_META)[:, None]
    mj = jnp.arange(N_EXTRA)[None, :]
    mbias = stacked_bias(N_META, mj <= mi)
    return pl.pallas_call(
        _attn_kernel,
        out_shape=jax.ShapeDtypeStruct((bsz, t_len, D_ATT), BF16),
        grid=(bsz,),
        in_specs=[
            pl.BlockSpec((1, t_len, ATT_COLS), lambda b: (b, 0, 0)),
            _const_spec((t_len, LANES)),
            _const_spec((t_len, LANES)),
            _const_spec((2, N_KV_HEADS, Q_ROWS, N_KEYS)),
            _const_spec((N_KV_HEADS, GQA_GROUP * N_META, N_EXTRA)),
        ],
        out_specs=pl.BlockSpec((1, t_len, D_ATT), lambda b: (b, 0, 0)),
        scratch_shapes=[
            pltpu.VMEM((N_Q_HEADS, t_len, HEAD_DIM), BF16),
            pltpu.VMEM((N_KV_HEADS, K_PAD + t_len, HEAD_DIM), BF16),
            pltpu.VMEM((N_KV_HEADS, K_PAD + t_len, V_WIDTH), BF16),
            pltpu.VMEM((N_KV_HEADS, N_EXTRA, HEAD_DIM), BF16),
            pltpu.VMEM((N_KV_HEADS, N_EXTRA, V_WIDTH), BF16),
        ],
        compiler_params=pltpu.CompilerParams(
            dimension_semantics=("arbitrary",),
            vmem_limit_bytes=VMEM_LIMIT_BYTES),
        name="swa_attention",
    )(qkv, cos, sin, bias, mbias)


def _rwkv_kernel(pr_ref, mu_ref, w0_ref, a0_ref, wlora_ref, g2_ref, kk_ref, ka_ref, rk_ref,
                 lng_ref, lnb_ref, hsum_ref, mexp_ref, mbd_ref, mpair_ref, tril_ref,
                 o_ref,
                 prev_ref, state_ref, logw_ref, r_ref, kn_ref, b_ref, k2_ref, v_ref,
                 rp_ref, yi_ref, gm_ref, hm_ref, y_ref, bonus_ref, gate_ref):
    tm = pr_ref.shape[1]
    i = pl.program_id(1)
    L = WKV_CHUNK

    @pl.when(i == 0)
    def _():
        prev_ref[...] = jnp.zeros(prev_ref.shape, F32)
        state_ref[...] = jnp.zeros(state_ref.shape, F32)

    x = pr_ref[0]
    row = lax.broadcasted_iota(jnp.int32, (tm, 1), 0)
    prev = jnp.where(row == 0, prev_ref[...], pltpu.roll(x, 1, axis=0))
    prev_ref[...] = x[tm - 1:tm, :]
    xm = x + (prev - x) * mu_ref[...]
    r = xm[:, 0:D_R]
    k = xm[:, D_R:2 * D_R]
    v = xm[:, 2 * D_R:3 * D_R]
    lora_in = xm[:, 3 * D_R:3 * D_R + LORA_W + LORA_A]
    lane = lax.broadcasted_iota(jnp.int32, lora_in.shape, 1)
    lora_in = jnp.where(lane < LORA_W, jnp.tanh(lora_in), lora_in).astype(BF16)
    lora = _dot(lora_in, wlora_ref[...])
    z = -(w0_ref[...] + lora[:, 0:D_R])
    softplus = jnp.maximum(z, 0.0) + jnp.log(1.0 + jnp.exp(-jnp.abs(z)))
    logw = -jnp.exp(-softplus - 0.5)
    alpha = _sigmoid(a0_ref[...] + lora[:, D_R:2 * D_R])
    gd = xm[:, 3 * D_R + LORA_W + LORA_A:RWKV_COLS]
    gate_ref[...] = _dot(_sigmoid(gd).astype(BF16), g2_ref[...])

    kk = k * kk_ref[...]
    norm = jnp.sqrt(_split_dot(kk * kk, hsum_ref[...]))
    kk = kk / jnp.maximum(norm, 1e-12)
    k2 = k * (1.0 + (alpha - 1.0) * ka_ref[...])
    bonus_ref[...] = _split_dot(r * k2 * rk_ref[...], hsum_ref[...]) * v
    logw_ref[...] = logw
    r_ref[...] = r
    kn_ref[...] = kk
    b_ref[...] = kk * alpha
    k2_ref[...] = k2
    v_ref[...] = v.astype(BF16)

    s_idx = lax.broadcasted_iota(jnp.int32, (L, LANES), 0)
    r_idx = lax.broadcasted_iota(jnp.int32, (L, LANES), 1) % L
    strict = r_idx < s_idx
    incl = r_idx <= s_idx
    eye_p = (r_idx == s_idx).astype(F32)
    d_row = lax.broadcasted_iota(jnp.int32, (LANES, LANES), 0)
    d_col = lax.broadcasted_iota(jnp.int32, (LANES, LANES), 1)
    diag = d_row == d_col
    n_rep = LANES // L
    tril = tril_ref[...]

    def expand(xb):
        return jnp.concatenate([xb] * n_rep, axis=0) * mexp_ref[...]

    def bdot(lhs, yp):
        ybd = jnp.concatenate([yp.astype(BF16)] * n_rep, axis=0) * mbd_ref[...]
        return _dot(lhs.astype(BF16), ybd)

    def stack(top, bottom):
        return jnp.concatenate([top.astype(BF16), bottom.astype(BF16)], axis=0)

    def advance(c, slot, j):
        def step():
            r0 = _scaled(c, L)
            for g in range(N_GROUPS):
                sl = slice(g * LANES, (g + 1) * LANES)
                s2 = state_ref[g].astype(BF16)
                out = _dot(jnp.concatenate([rp_ref[slot, j * L:(j + 1) * L, sl], gm_ref[slot, j, g]],
                                           axis=0), s2)
                y_ref[pl.ds(r0, L), sl] = out[0:L, :] + yi_ref[slot, j * L:(j + 1) * L, sl]
                state_ref[g] = out[L:, :] + hm_ref[slot, j, g]
        return step

    def process(chunks, slot, pending):
        pending = list(pending)

        def tick():
            if pending:
                pending.pop(0)()

        n = range(len(chunks))
        r0 = [_scaled(c, L) for c in chunks]
        at, rt, bt, kt, bh, kh, vv, wl = [], [], [], [], [], [], [], []
        for r in r0:
            lw = logw_ref[pl.ds(r, L), :]
            g_inc = _split_dot_lhs(tril, lw)
            e_in = jnp.exp(g_inc)
            e_neg = jnp.exp(-g_inc)
            e_last = jnp.exp(g_inc[L - 1:L, :] - g_inc)
            kn = kn_ref[pl.ds(r, L), :]
            b = b_ref[pl.ds(r, L), :]
            k2 = k2_ref[pl.ds(r, L), :]
            at.append((-kn * jnp.exp(g_inc - lw)).astype(BF16))
            rt.append(r_ref[pl.ds(r, L), :] * e_in)
            bt.append((b * e_neg).astype(BF16))
            kt.append((k2 * e_neg).astype(BF16))
            bh.append((b * e_last).astype(BF16))
            kh.append((k2 * e_last).astype(BF16))
            vv.append(v_ref[pl.ds(r, L), :])
            wl.append(e_in[L - 1:L, :])
        tick()
        sc = [_dot_nt(stack(at[i], rt[i]),
                      jnp.concatenate([expand(bt[i]), expand(kt[i])], axis=0)) for i in n]
        tick()
        a_ab = [jnp.where(strict, s[0:L, 0:LANES], 0.0) for s in sc]
        a_ak = [jnp.where(strict, s[0:L, LANES:2 * LANES], 0.0) for s in sc]
        m_rb = [jnp.where(incl, s[L:2 * L, 0:LANES], 0.0) for s in sc]
        m_rk = [jnp.where(incl, s[L:2 * L, LANES:2 * LANES], 0.0) for s in sc]
        p = [eye_p + a for a in a_ab]
        apow = [bdot(a, a) for a in a_ab]
        tick()
        for _ in range(2):
            st = [bdot(stack(p[i], apow[i]), apow[i]) for i in n]
            p = [p[i] + st[i][0:L, :] for i in n]
            apow = [s[L:2 * L, :] for s in st]
            tick()
        p = [p[i] + bdot(p[i], apow[i]) for i in n]
        tick()
        mp = [bdot(m_rb[i], p[i]) for i in n]
        tick()
        pm = [stack(p[i], mp[i]) for i in n]
        st = [bdot(pm[i], a_ak[i]) for i in n]
        pq = [stack(st[i][0:L, :], st[i][L:2 * L, :] + m_rk[i]) for i in n]
        tick()
        st = [_dot(pm[i], expand(at[i])) for i in n]
        atp = [s[0:L, :].astype(BF16) for s in st]
        for i in n:
            rp_ref[slot, i * L:(i + 1) * L, :] = (rt[i] + st[i][L:2 * L, :]).astype(BF16)
        tick()
        st = [_dot(pq[i], expand(vv[i])) for i in n]
        uv = [s[0:L, :].astype(BF16) for s in st]
        for i in n:
            yi_ref[slot, i * L:(i + 1) * L, :] = st[i][L:2 * L, :]
        tick()
        for i in n:
            for g in range(N_GROUPS):
                sl = slice(g * LANES, (g + 1) * LANES)
                rhs = jnp.concatenate(
                    [jnp.concatenate([atp[i][:, sl], uv[i][:, sl]], axis=1),
                     jnp.concatenate([jnp.zeros((L, LANES), BF16), vv[i][:, sl]], axis=1)], axis=0)
                out = _dot_tn(jnp.concatenate([bh[i][:, sl], kh[i][:, sl]], axis=0), rhs)
                gm = out[:, 0:LANES] * mpair_ref[...] + jnp.where(diag, wl[i][:, sl], 0.0)
                gm_ref[slot, i, g] = gm.astype(BF16)
                hm_ref[slot, i, g] = out[:, LANES:2 * LANES] * mpair_ref[...]
            tick()
        while pending:
            pending.pop(0)()

    n_chunks = tm // L
    n_full = n_chunks // WKV_GROUP
    rem = n_chunks % WKV_GROUP
    process(list(range(WKV_GROUP)), 0, [])

    def trip(t, carry):
        slot = t % 2
        process([t * WKV_GROUP + u for u in range(WKV_GROUP)], slot,
                [advance((t - 1) * WKV_GROUP + u, 1 - slot, u) for u in range(WKV_GROUP)])
        return carry

    lax.fori_loop(1, n_full, trip, 0)
    last = (n_full - 1) % 2
    process([n_full * WKV_GROUP + u for u in range(rem)], 1 - last,
            [advance((n_full - 1) * WKV_GROUP + u, last, u) for u in range(WKV_GROUP)])
    process([], 1 - last, [advance(n_full * WKV_GROUP + u, 1 - last, u) for u in range(rem)])

    y = y_ref[...]
    mean = _split_dot(y, hsum_ref[...]) * (1.0 / RWKV_HEAD)
    cen = y - mean
    var = _split_dot(cen * cen, hsum_ref[...]) * (1.0 / RWKV_HEAD)
    yn = cen * lax.rsqrt(var + RWKV_GN_EPS) * lng_ref[...] + lnb_ref[...]
    o_ref[0] = ((yn + bonus_ref[...]) * gate_ref[...]).astype(BF16)


def _rwkv(pr, mu, w0, a0, w_lora, g2, k_k, k_a, r_k, lnx_g, lnx_b):
    bsz, t_len, _ = pr.shape
    tm = ROW_TILE
    nt = t_len // tm
    assert (tm // WKV_CHUNK) // WKV_GROUP >= 1 and tm % WKV_CHUNK == 0
    group_rows = WKV_GROUP * WKV_CHUNK

    lane_head = jnp.arange(D_R) // RWKV_HEAD
    hsum = (lane_head[:, None] == lane_head[None, :]).astype(BF16)
    row_head = jnp.arange(LANES) // WKV_CHUNK
    mexp = (row_head[:, None] == lane_head[None, :]).astype(BF16)
    mbd = (row_head[:, None] == row_head[None, :]).astype(BF16)
    pair_head = jnp.arange(LANES) // RWKV_HEAD
    mpair = (pair_head[:, None] == pair_head[None, :]).astype(F32)
    tril = jnp.tril(jnp.ones((WKV_CHUNK, WKV_CHUNK), F32)).astype(BF16)

    tile_f32 = pltpu.VMEM((tm, D_R), F32)
    tile_bf16 = pltpu.VMEM((tm, D_R), BF16)
    return pl.pallas_call(
        _rwkv_kernel,
        out_shape=jax.ShapeDtypeStruct((bsz, t_len, D_R), BF16),
        grid=(bsz, nt),
        in_specs=[
            pl.BlockSpec((1, tm, RWKV_COLS), lambda b, i: (b, i, 0)),
            _const_spec((1, RWKV_COLS)),
            _const_spec((1, D_R)),
            _const_spec((1, D_R)),
            _const_spec((LORA_W + LORA_A, 2 * D_R)),
            _const_spec((LORA_G, D_R)),
            _const_spec((1, D_R)),
            _const_spec((1, D_R)),
            _const_spec((1, D_R)),
            _const_spec((1, D_R)),
            _const_spec((1, D_R)),
            _const_spec((D_R, D_R)),
            _const_spec((LANES, D_R)),
            _const_spec((LANES, LANES)),
            _const_spec((LANES, LANES)),
            _const_spec((WKV_CHUNK, WKV_CHUNK)),
        ],
        out_specs=pl.BlockSpec((1, tm, D_R), lambda b, i: (b, i, 0)),
        scratch_shapes=[
            pltpu.VMEM((1, RWKV_COLS), F32),
            pltpu.VMEM((N_GROUPS, LANES, LANES), F32),
            tile_f32, tile_f32, tile_f32, tile_f32, tile_f32, tile_bf16,
            pltpu.VMEM((2, group_rows, D_R), BF16),
            pltpu.VMEM((2, group_rows, D_R), F32),
            pltpu.VMEM((2, WKV_GROUP, N_GROUPS, LANES, LANES), BF16),
            pltpu.VMEM((2, WKV_GROUP, N_GROUPS, LANES, LANES), F32),
            tile_f32, tile_f32, tile_f32,
        ],
        compiler_params=pltpu.CompilerParams(
            dimension_semantics=("arbitrary", "arbitrary"),
            vmem_limit_bytes=VMEM_LIMIT_BYTES),
        name="rwkv7",
    )(pr, mu, w0, a0, w_lora, g2, k_k, k_a, r_k, lnx_g, lnx_b, hsum, mexp, mbd, mpair, tril)


def _rope_tables(t_len):
    half = HEAD_DIM // 2
    inv = ROPE_THETA ** (-jnp.arange(half, dtype=F32) / half)
    ang = jnp.arange(t_len, dtype=F32)[:, None] * inv[None, :]
    cos = jnp.cos(ang)
    sin = jnp.sin(ang)
    reps = LANES // HEAD_DIM
    cos_t = jnp.tile(jnp.concatenate([cos, cos], axis=1), (1, reps))
    sin_t = jnp.tile(jnp.concatenate([-sin, sin], axis=1), (1, reps))
    return cos_t, sin_t


def kernel(x, meta_tokens, norm_mix, norm_ffn, norm_final, ev_w_in, ev_conv_a, ev_ln_a_g, ev_ln_a_b, ev_conv_b, ev_w_out, od_w_in, od_sinks, od_mu, od_w0, od_w2, od_a0, od_a2, od_g2, od_k_k, od_k_a, od_r_k, od_lnx_g, od_lnx_b, od_w_out, ff_w_up, ff_conv, ff_conv_b, ff_w_down):
    bsz, seq, d = x.shape
    depth = norm_mix.shape[0]
    t_len = N_META + seq
    meta = jnp.broadcast_to(meta_tokens[None].astype(x.dtype), (bsz, N_META, d))
    h = jnp.concatenate([meta, x], axis=1)
    row = lambda v: v.reshape(1, -1).astype(F32)
    cos_t, sin_t = _rope_tables(t_len)
    for i in range(depth):
        j = i // 2
        mixer_out = None
        if i % 2 == 0:
            h = _even_mixer(h, row(norm_mix[i]), ev_w_in[j].astype(BF16),
                            jnp.repeat(ev_conv_a[j], SUBLANES, axis=0),
                            row(ev_ln_a_g[j]), row(ev_ln_a_b[j]), ev_conv_b[j],
                            ev_w_out[j].astype(BF16))
        else:
            h2d = h.reshape(bsz * t_len, d)
            qkv, pr = _odd_in_proj(h2d, row(norm_mix[i]), od_w_in[j].astype(BF16))
            y_att = _attention(qkv.reshape(bsz, t_len, ATT_COLS), cos_t, sin_t, row(od_sinks[j]))
            zeros = jnp.zeros((LORA_W, D_R), F32)
            w_lora = jnp.concatenate(
                [jnp.concatenate([od_w2[j], zeros], axis=1),
                 jnp.concatenate([zeros, od_a2[j]], axis=1)], axis=0).astype(BF16)
            y_rwkv = _rwkv(pr.reshape(bsz, t_len, RWKV_COLS), row(od_mu[j]), row(od_w0[j]),
                           row(od_a0[j]), w_lora, od_g2[j].astype(BF16), row(od_k_k[j]),
                           row(od_k_a[j]), row(od_r_k[j]), row(od_lnx_g[j]), row(od_lnx_b[j]))
            mixer_out = (y_att, y_rwkv, od_w_out[j].astype(BF16))
        last = i == depth - 1
        h = _ffn(h, row(norm_ffn[i]), ff_w_up[i].astype(BF16), ff_conv[i], row(ff_conv_b[i]),
                 ff_w_down[i].astype(BF16), row(norm_final), final_norm=last,
                 mixer_out=mixer_out)
    return h[:, N_META:]
```

```python
import functools

import jax
import jax.numpy as jnp
from jax import lax
from jax.experimental import pallas as pl
from jax.experimental.pallas import tpu as pltpu

D_MODEL = 1024
N_META = 16
RMS_EPS = 1e-6
LN_EPS = 1e-5
D_A = 512
D_B = 512
CONV_A_WIDTH = 31
CONV_B_WIDTH = 3
HEAD_DIM = 64
N_Q_HEADS = 8
N_KV_HEADS = 2
GQA_GROUP = N_Q_HEADS // N_KV_HEADS
D_ATT = N_Q_HEADS * HEAD_DIM
D_KV = N_KV_HEADS * HEAD_DIM
WINDOW = 128
BLOCK = 128
ROPE_THETA = 10000.0
D_R = 512
N_R_HEADS = 8
RWKV_HEAD = 64
LORA_W = 64
LORA_A = 64
LORA_G = 128
RWKV_GN_EPS = 64e-5
ATT_COLS = D_ATT + 2 * D_KV
RWKV_COLS = 3 * D_R + LORA_W + LORA_A + LORA_G
D_FF = 2816
NEG_INF = -1e30

VMEM_LIMIT_BYTES = 56 * 1024 * 1024
LANES = 128
SUBLANES = 8
ROW_TILE = 688
FINAL_ROW_TILE = 512
CONV_ROWS = 16
CONV_GROUP = 4
FF_CHUNK = 256
WKV_CHUNK = 16
WKV_GROUP = 8
HEADS_PER_GROUP = LANES // RWKV_HEAD
N_GROUPS = N_R_HEADS // HEADS_PER_GROUP

F32 = jnp.float32
BF16 = jnp.bfloat16


def _const_spec(shape):
    nd = len(shape)
    return pl.BlockSpec(shape, lambda *_: (0,) * nd, pipeline_mode=pl.Buffered(1))


def _rms_norm(x, g):
    ms = jnp.mean(x * x, axis=-1, keepdims=True)
    return x * lax.rsqrt(ms + RMS_EPS) * g


def _sigmoid(x):
    return 1.0 / (1.0 + jnp.exp(-x))


def _dot(a, b):
    return jnp.dot(a, b, preferred_element_type=F32)


def _dot_nt(a, b):
    return lax.dot_general(a, b, (((1,), (1,)), ((), ())), preferred_element_type=F32)


def _dot_tn(a, b):
    return lax.dot_general(a, b, (((0,), (0,)), ((), ())), preferred_element_type=F32)


def _scaled(i, m):
    return i * m if isinstance(i, int) else pl.multiple_of(i * m, m)


def _grouped_loop(n, group, body):
    def trip(t, carry):
        body([t * group + u for u in range(group)])
        return carry

    lax.fori_loop(0, n // group, trip, 0)
    if n % group:
        body(list(range(n - n % group, n)))


def _split_dot(x, w_bf16):
    hi = x.astype(BF16)
    lo = (x - hi.astype(F32)).astype(BF16)
    return _dot(hi, w_bf16) + _dot(lo, w_bf16)


def _split_dot_lhs(w_bf16, x):
    hi = x.astype(BF16)
    lo = (x - hi.astype(F32)).astype(BF16)
    return _dot(w_bf16, hi) + _dot(w_bf16, lo)


A_HIST = 32
B_HIST = 8


def _even_mixer_kernel(*refs, prepend_meta):
    if prepend_meta:
        h_ref, meta_ref = refs[:2]
        refs = refs[1:]
    else:
        h_ref = refs[0]
    _, g_ref, win_ref, ca_ref, lng_ref, lnb_ref, cb_ref, wout_ref, o_ref = refs[:9]
    ua_ref, ush_ref, ub_ref, y_ref = refs[9:13]
    tm = h_ref.shape[1]
    i = pl.program_id(1)

    @pl.when(i == 0)
    def _():
        ua_ref[0:A_HIST, :] = jnp.zeros((A_HIST, D_A), F32)
        ub_ref[0:B_HIST, :] = jnp.zeros((B_HIST, D_B), F32)
        ua_ref[A_HIST + tm:A_HIST + tm + SUBLANES, :] = jnp.zeros((SUBLANES, D_A), F32)

    @pl.when(i > 0)
    def _():
        ua_ref[0:A_HIST, :] = ua_ref[tm:tm + A_HIST, :]
        ub_ref[0:B_HIST, :] = ub_ref[tm:tm + B_HIST, :]

    if prepend_meta:
        xt_ref = refs[13]

        @pl.when(i == 0)
        def _():
            xt_ref[0:N_META, :] = meta_ref[...]
            xt_ref[N_META:tm, :] = h_ref[0, 0:tm - N_META, :]

        @pl.when(i > 0)
        def _():
            xt_ref[...] = h_ref[0]

        x = xt_ref[...]
    else:
        x = h_ref[0]
    hn = _rms_norm(x, g_ref[...]).astype(BF16)
    a_val = _dot(hn, win_ref[:, 0:D_A])
    a_gate = _dot(hn, win_ref[:, D_A:2 * D_A])
    ua_ref[A_HIST:A_HIST + tm, :] = a_val * _sigmoid(a_gate)
    g_c = _dot(hn, win_ref[:, 2 * D_A + D_B:2 * D_A + 2 * D_B])
    x_in = _dot(hn, win_ref[:, 2 * D_A + 2 * D_B:2 * D_A + 3 * D_B])
    ub_ref[B_HIST:B_HIST + tm, :] = g_c * x_in

    conv_b = jnp.zeros((tm, D_B), F32)
    for j in range(CONV_B_WIDTH):
        off = B_HIST - (CONV_B_WIDTH - 1) + j
        conv_b = conv_b + cb_ref[j:j + 1, :] * ub_ref[off:off + tm, :]
    g_b = _dot(hn, win_ref[:, 2 * D_A:2 * D_A + D_B])
    y_ref[:, D_A:D_A + D_B] = (g_b * conv_b).astype(BF16)

    n_sh = ush_ref.shape[1]
    for s in range(SUBLANES):
        ush_ref[s] = ua_ref[s:s + n_sh, :]

    lng = lng_ref[...]
    lnb = lnb_ref[...]
    first = A_HIST - (CONV_A_WIDTH - 1)

    def conv_blocks(blocks):
        starts = [_scaled(ci, CONV_ROWS) for ci in blocks]
        accs = [jnp.zeros((CONV_ROWS, D_A), F32) for _ in blocks]
        for j in range(CONV_A_WIDTH):
            q, s = divmod(first + j, SUBLANES)
            w8 = ca_ref[j * SUBLANES:(j + 1) * SUBLANES, :]
            w = jnp.concatenate([w8] * (CONV_ROWS // SUBLANES), axis=0)
            for u, r0 in enumerate(starts):
                accs[u] = accs[u] + w * ush_ref[s, pl.ds(r0 + q * SUBLANES, CONV_ROWS), :]
        for acc, r0 in zip(accs, starts):
            mu = jnp.mean(acc, axis=-1, keepdims=True)
            cen = acc - mu
            var = jnp.mean(cen * cen, axis=-1, keepdims=True)
            ya = cen * lax.rsqrt(var + LN_EPS) * lng + lnb
            ya = ya * _sigmoid(ya)
            y_ref[pl.ds(r0, CONV_ROWS), 0:D_A] = ya.astype(BF16)

    n_blocks = tm // CONV_ROWS
    for b0 in range(0, n_blocks, CONV_GROUP):
        conv_blocks(list(range(b0, min(b0 + CONV_GROUP, n_blocks))))
    o_ref[0] = x + _dot(y_ref[...], wout_ref[...])


def _even_mixer(h, g, w_in, conv_a, ln_g, ln_b, conv_b, w_out, meta=None):
    bsz, rows_in, d = h.shape
    tm = ROW_TILE
    n_in = w_in.shape[1]
    scratch = [
        pltpu.VMEM((A_HIST + tm + SUBLANES, D_A), F32),
        pltpu.VMEM((SUBLANES, A_HIST + tm, D_A), F32),
        pltpu.VMEM((B_HIST + tm, D_B), F32),
        pltpu.VMEM((tm, D_A + D_B), BF16),
    ]
    if meta is None:
        t_len = rows_in
        h_specs = [pl.BlockSpec((1, tm, d), lambda b, i: (b, i, 0))]
        h_args = (h,)
    else:
        t_len = rows_in + N_META
        h_specs = [pl.BlockSpec((pl.Element(1), pl.Element(tm), pl.Element(d)),
                                lambda b, i: (b, pl.multiple_of(
                                    jnp.maximum(i * tm - N_META, 0), SUBLANES), 0)),
                   _const_spec((N_META, d))]
        h_args = (h, meta)
        scratch.append(pltpu.VMEM((tm, d), F32))
    nt = t_len // tm
    return pl.pallas_call(
        functools.partial(_even_mixer_kernel, prepend_meta=meta is not None),
        out_shape=jax.ShapeDtypeStruct((bsz, t_len, d), F32),
        grid=(bsz, nt),
        in_specs=[
            *h_specs,
            _const_spec((1, d)),
            _const_spec((d, n_in)),
            _const_spec((CONV_A_WIDTH * SUBLANES, D_A)),
            _const_spec((1, D_A)),
            _const_spec((1, D_A)),
            _const_spec((CONV_B_WIDTH, D_B)),
            _const_spec((D_A + D_B, d)),
        ],
        out_specs=pl.BlockSpec((1, tm, d), lambda b, i: (b, i, 0)),
        scratch_shapes=scratch,
        compiler_params=pltpu.CompilerParams(
            dimension_semantics=("arbitrary", "arbitrary"),
            vmem_limit_bytes=VMEM_LIMIT_BYTES),
        name="even_mixer",
    )(*h_args, g, w_in, conv_a, ln_g, ln_b, conv_b, w_out)


F_HIST = 8


def _ffn_kernel(*refs, final_norm, mixer_out, halo):
    if mixer_out:
        h_ref, ya_ref, yr_ref, wo_ref = refs[:4]
        refs = refs[3:]
    else:
        h_ref = refs[0]
    _, g_ref, wup_ref, cw_ref, cb_ref, wdn_ref, gf_ref, o_ref = refs[:8]
    if halo:
        (act_ref,) = refs[8:]
    else:
        carry_ref, act_ref = refs[8:]
    tm = h_ref.shape[1]
    i = pl.program_id(1)
    n_chunks = D_FF // FF_CHUNK

    if not halo:
        @pl.when(i == 0)
        def _():
            carry_ref[...] = jnp.zeros(carry_ref.shape, F32)

    x = h_ref[0]
    if mixer_out:
        x = (x + _dot(ya_ref[0], wo_ref[0:D_ATT, :]) + _dot(yr_ref[0], wo_ref[D_ATT:D_ATT + D_R, :]))
    hn = _rms_norm(x, g_ref[...]).astype(BF16)

    def up_proj(c):
        c0 = c * FF_CHUNK
        return (_dot(hn, wup_ref[:, c0:c0 + FF_CHUNK]),
                _dot(hn, wup_ref[:, D_FF + c0:D_FF + c0 + FF_CHUNK]))

    nxt = up_proj(0)
    for c in range(n_chunks):
        g_raw, val = nxt
        if c + 1 < n_chunks:
            nxt = up_proj(c + 1)
        cols = slice(c * FF_CHUNK, (c + 1) * FF_CHUNK)
        if halo:
            prev1 = pltpu.roll(g_raw, 1, axis=0)
            prev2 = pltpu.roll(g_raw, 2, axis=0)
        else:
            ext = jnp.concatenate([carry_ref[:, cols], g_raw], axis=0)
            carry_ref[:, cols] = g_raw[tm - F_HIST:tm, :]
            prev1 = pltpu.roll(ext, 1, axis=0)[F_HIST:, :]
            prev2 = pltpu.roll(ext, 2, axis=0)[F_HIST:, :]
        gate = (cb_ref[:, cols] + cw_ref[2:3, cols] * g_raw
                + cw_ref[1:2, cols] * prev1 + cw_ref[0:1, cols] * prev2)
        act_ref[:, cols] = (gate * _sigmoid(gate) * val).astype(BF16)

    out = x + _dot(act_ref[...], wdn_ref[...])
    if final_norm:
        out = _rms_norm(out, gf_ref[...])
    o_ref[0] = out[halo:, :]


def _ffn(h, g, w_up, conv_w, conv_b, w_down, g_final, final_norm, mixer_out=None,
         drop_meta=False):
    bsz, t_len, d = h.shape
    if drop_meta:
        halo = N_META
        out_rows = FINAL_ROW_TILE
        nt = (t_len - N_META) // out_rows
        tm = out_rows + halo

        def rows_spec(width):
            return pl.BlockSpec((pl.Element(1), pl.Element(tm), pl.Element(width)),
                                lambda b, i: (b, i * out_rows, 0))
        out_len = t_len - N_META
    else:
        halo = 0
        out_rows = tm = ROW_TILE
        nt = t_len // tm

        def rows_spec(width):
            return pl.BlockSpec((1, tm, width), lambda b, i: (b, i, 0))
        out_len = t_len
    mix_args, mix_specs = (), []
    if mixer_out is not None:
        y_att, y_rwkv, w_out = mixer_out
        mix_args = (y_att, y_rwkv, w_out)
        mix_specs = [rows_spec(D_ATT), rows_spec(D_R), _const_spec((D_ATT + D_R, d))]
    scratch = [] if halo else [pltpu.VMEM((F_HIST, D_FF), F32)]
    return pl.pallas_call(
        functools.partial(_ffn_kernel, final_norm=final_norm, mixer_out=mixer_out is not None,
                          halo=halo),
        out_shape=jax.ShapeDtypeStruct((bsz, out_len, d), F32),
        grid=(bsz, nt),
        in_specs=[
            rows_spec(d),
            *mix_specs,
            _const_spec((1, d)),
            _const_spec((d, 2 * D_FF)),
            _const_spec((3, D_FF)),
            _const_spec((1, D_FF)),
            _const_spec((D_FF, d)),
            _const_spec((1, d)),
        ],
        out_specs=pl.BlockSpec((1, out_rows, d), lambda b, i: (b, i, 0)),
        scratch_shapes=scratch + [pltpu.VMEM((tm, D_FF), BF16)],
        compiler_params=pltpu.CompilerParams(
            dimension_semantics=("arbitrary", "arbitrary"),
            vmem_limit_bytes=VMEM_LIMIT_BYTES),
        name="ffn_final" if final_norm else "ffn",
    )(h, *mix_args, g, w_up, conv_w, conv_b, w_down, g_final)


def _odd_in_kernel(h_ref, g_ref, w_ref, qkv_ref, pr_ref):
    hn = _rms_norm(h_ref[...], g_ref[...]).astype(BF16)
    qkv_ref[...] = _dot(hn, w_ref[:, 0:ATT_COLS])
    pr_ref[...] = _dot(hn, w_ref[:, ATT_COLS:ATT_COLS + RWKV_COLS])


def _odd_in_proj(h2d, g, w_in):
    rows, d = h2d.shape
    tm = ROW_TILE
    return pl.pallas_call(
        _odd_in_kernel,
        out_shape=(jax.ShapeDtypeStruct((rows, ATT_COLS), F32),
                   jax.ShapeDtypeStruct((rows, RWKV_COLS), F32)),
        grid=(rows // tm,),
        in_specs=[
            pl.BlockSpec((tm, d), lambda i: (i, 0)),
            _const_spec((1, d)),
            _const_spec((d, ATT_COLS + RWKV_COLS)),
        ],
        out_specs=(pl.BlockSpec((tm, ATT_COLS), lambda i: (i, 0)),
                   pl.BlockSpec((tm, RWKV_COLS), lambda i: (i, 0))),
        compiler_params=pltpu.CompilerParams(
            dimension_semantics=("arbitrary",),
            vmem_limit_bytes=VMEM_LIMIT_BYTES),
        name="odd_in_proj",
    )(h2d, g, w_in)


def _rope(x, cos, sin):
    lane = lax.broadcasted_iota(jnp.int32, x.shape, 1)
    first_half = (lane % HEAD_DIM) < (HEAD_DIM // 2)
    rot = jnp.where(first_half,
                    pltpu.roll(x, LANES - HEAD_DIM // 2, axis=1),
                    pltpu.roll(x, HEAD_DIM // 2, axis=1))
    return x * cos + rot * sin


K_PAD = BLOCK - N_META
N_EXTRA = 2 * N_META
N_KEYS = 2 * BLOCK + N_EXTRA
SINK_COL = 2 * BLOCK + N_META
Q_ROWS = GQA_GROUP * BLOCK
ATTN_GROUP = 4
V_WIDTH = 2 * LANES


def _attn_kernel(qkv_ref, cos_ref, sin_ref, bias_ref, mbias_ref, o_ref,
                 q_ref, k_ref, v_ref, kx_ref, vx_ref):
    t_len = qkv_ref.shape[1]
    scale = HEAD_DIM ** -0.5
    rt = ROW_TILE
    heads_per_group = LANES // HEAD_DIM

    k_ref[:, 0:K_PAD, :] = jnp.zeros((N_KV_HEADS, K_PAD, HEAD_DIM), BF16)
    v_ref[:, 0:K_PAD, :] = jnp.zeros((N_KV_HEADS, K_PAD, V_WIDTH), BF16)

    def rope_rows(ti, carry):
        r0 = pl.multiple_of(ti * rt, 16)
        cos = cos_ref[pl.ds(r0, rt), :]
        sin = sin_ref[pl.ds(r0, rt), :]
        for gq in range(D_ATT // LANES):
            xq = qkv_ref[0, pl.ds(r0, rt), gq * LANES:(gq + 1) * LANES]
            xq = (_rope(xq, cos, sin) * scale).astype(BF16)
            for u in range(heads_per_group):
                q_ref[gq * heads_per_group + u, pl.ds(r0, rt), :] = xq[:, u * HEAD_DIM:(u + 1) * HEAD_DIM]
        xk = _rope(qkv_ref[0, pl.ds(r0, rt), D_ATT:D_ATT + D_KV], cos, sin).astype(BF16)
        xv = qkv_ref[0, pl.ds(r0, rt), D_ATT + D_KV:ATT_COLS].astype(BF16)
        ones = jnp.ones((rt, LANES), BF16)
        for u in range(N_KV_HEADS):
            k_ref[u, pl.ds(K_PAD + r0, rt), :] = xk[:, u * HEAD_DIM:(u + 1) * HEAD_DIM]
            vu = xv[:, u * HEAD_DIM:(u + 1) * HEAD_DIM]
            v_ref[u, pl.ds(K_PAD + r0, rt), :] = jnp.concatenate([vu, vu, ones], axis=1)
        return carry

    lax.fori_loop(0, t_len // rt, rope_rows, 0)

    kx_ref[:, 0:N_META, :] = k_ref[:, K_PAD:K_PAD + N_META, :]
    kx_ref[:, N_META:N_EXTRA, :] = jnp.zeros((N_KV_HEADS, N_EXTRA - N_META, HEAD_DIM), BF16)
    vx_ref[:, 0:N_META, :] = v_ref[:, K_PAD:K_PAD + N_META, :]
    vx_ref[:, N_META:N_EXTRA, :] = jnp.concatenate(
        [jnp.zeros((N_KV_HEADS, N_EXTRA - N_META, LANES), BF16),
         jnp.ones((N_KV_HEADS, N_EXTRA - N_META, LANES), BF16)], axis=2)

    def softmax_pv(items, n_q):
        s = []
        for g, q_start, keys, _, bias in items:
            q4 = q_ref[g * GQA_GROUP:(g + 1) * GQA_GROUP, pl.ds(q_start, n_q), :]
            s.append(_dot_nt(q4.reshape(GQA_GROUP * n_q, HEAD_DIM), keys) + bias)
        p = [jnp.exp(x - jnp.max(x, axis=-1, keepdims=True)).astype(BF16) for x in s]
        r = [_dot(p[i], items[i][3]) for i in range(len(items))]
        lane = lax.broadcasted_iota(jnp.int32, (n_q, LANES), 1)
        for (g, q_start, _, _, _), ri in zip(items, r):
            o = ri[:, 0:LANES] / ri[:, LANES:V_WIDTH]
            for u in range(GQA_GROUP // heads_per_group):
                pair = jnp.where(lane < HEAD_DIM, o[2 * u * n_q:(2 * u + 1) * n_q, :],
                                 o[(2 * u + 1) * n_q:(2 * u + 2) * n_q, :])
                col = (g * GQA_GROUP + 2 * u) * HEAD_DIM
                o_ref[0, pl.ds(q_start, n_q), col:col + LANES] = pair.astype(BF16)

    softmax_pv([(g, 0, kx_ref[g], vx_ref[g], mbias_ref[g]) for g in range(N_KV_HEADS)], N_META)

    def blocks(ns):
        items = []
        for n in ns:
            band = _scaled(n, BLOCK)
            variant = min(n, 1) if isinstance(n, int) else jnp.minimum(n, 1)
            for g in range(N_KV_HEADS):
                keys = jnp.concatenate([k_ref[g, pl.ds(band, 2 * BLOCK), :], kx_ref[g]], axis=0)
                vals = jnp.concatenate([v_ref[g, pl.ds(band, 2 * BLOCK), :], vx_ref[g]], axis=0)
                items.append((g, N_META + band, keys, vals, bias_ref[variant, g]))
        softmax_pv(items, BLOCK)

    _grouped_loop((t_len - N_META) // BLOCK, ATTN_GROUP, blocks)


def _attention(qkv, cos, sin, sinks):
    bsz, t_len, _ = qkv.shape
    def stacked_bias(n_q, visible):
        sink = sinks.reshape(N_KV_HEADS, GQA_GROUP, 1, 1).astype(F32)
        base = jnp.where(visible, 0.0, NEG_INF).astype(F32)
        col = jnp.arange(visible.shape[1])[None, :]
        tiled = jnp.broadcast_to(base[None, None], (N_KV_HEADS, GQA_GROUP) + base.shape)
        tiled = jnp.where(col == visible.shape[1] - N_EXTRA + N_META, sink, tiled)
        return tiled.reshape(N_KV_HEADS, GQA_GROUP * n_q, visible.shape[1])

    qi = jnp.arange(BLOCK)[:, None]
    kj = jnp.arange(N_KEYS)[None, :]
    in_window = (kj >= qi + 1) & (kj <= qi + WINDOW) & (kj < 2 * BLOCK)
    is_meta = (kj >= 2 * BLOCK) & (kj < 2 * BLOCK + N_META)
    bias = jnp.stack([stacked_bias(BLOCK, (in_window & (kj >= BLOCK)) | is_meta),
                      stacked_bias(BLOCK, in_window | is_meta)])
    mi = jnp.arange(N_META)[:, None]
    mj = jnp.arange(N_EXTRA)[None, :]
    mbias = stacked_bias(N_META, mj <= mi)
    return pl.pallas_call(
        _attn_kernel,
        out_shape=jax.ShapeDtypeStruct((bsz, t_len, D_ATT), BF16),
        grid=(bsz,),
        in_specs=[
            pl.BlockSpec((1, t_len, ATT_COLS), lambda b: (b, 0, 0)),
            _const_spec((t_len, LANES)),
            _const_spec((t_len, LANES)),
            _const_spec((2, N_KV_HEADS, Q_ROWS, N_KEYS)),
            _const_spec((N_KV_HEADS, GQA_GROUP * N_META, N_EXTRA)),
        ],
        out_specs=pl.BlockSpec((1, t_len, D_ATT), lambda b: (b, 0, 0)),
        scratch_shapes=[
            pltpu.VMEM((N_Q_HEADS, t_len, HEAD_DIM), BF16),
            pltpu.VMEM((N_KV_HEADS, K_PAD + t_len, HEAD_DIM), BF16),
            pltpu.VMEM((N_KV_HEADS, K_PAD + t_len, V_WIDTH), BF16),
            pltpu.VMEM((N_KV_HEADS, N_EXTRA, HEAD_DIM), BF16),
            pltpu.VMEM((N_KV_HEADS, N_EXTRA, V_WIDTH), BF16),
        ],
        compiler_params=pltpu.CompilerParams(
            dimension_semantics=("arbitrary",),
            vmem_limit_bytes=VMEM_LIMIT_BYTES),
        name="swa_attention",
    )(qkv, cos, sin, bias, mbias)


def _rwkv_kernel(pr_ref, mu_ref, w0_ref, a0_ref, wlora_ref, g2_ref, kk_ref, ka_ref, rk_ref,
                 lng_ref, lnb_ref, hsum_ref, mexp_ref, mbd_ref, mpair_ref, tril_ref,
                 o_ref,
                 prev_ref, state_ref, logw_ref, r_ref, kn_ref, b_ref, k2_ref, v_ref,
                 rp_ref, yi_ref, gm_ref, hm_ref, y_ref, bonus_ref, gate_ref):
    tm = pr_ref.shape[1]
    i = pl.program_id(1)
    L = WKV_CHUNK

    @pl.when(i == 0)
    def _():
        prev_ref[...] = jnp.zeros(prev_ref.shape, F32)
        state_ref[...] = jnp.zeros(state_ref.shape, F32)

    x = pr_ref[0]
    row = lax.broadcasted_iota(jnp.int32, (tm, 1), 0)
    prev = jnp.where(row == 0, prev_ref[...], pltpu.roll(x, 1, axis=0))
    prev_ref[...] = x[tm - 1:tm, :]
    xm = x + (prev - x) * mu_ref[...]
    r = xm[:, 0:D_R]
    k = xm[:, D_R:2 * D_R]
    v = xm[:, 2 * D_R:3 * D_R]
    lora_in = xm[:, 3 * D_R:3 * D_R + LORA_W + LORA_A]
    lane = lax.broadcasted_iota(jnp.int32, lora_in.shape, 1)
    lora_in = jnp.where(lane < LORA_W, jnp.tanh(lora_in), lora_in).astype(BF16)
    lora = _dot(lora_in, wlora_ref[...])
    z = -(w0_ref[...] + lora[:, 0:D_R])
    softplus = jnp.maximum(z, 0.0) + jnp.log(1.0 + jnp.exp(-jnp.abs(z)))
    logw = -jnp.exp(-softplus - 0.5)
    alpha = _sigmoid(a0_ref[...] + lora[:, D_R:2 * D_R])
    gd = xm[:, 3 * D_R + LORA_W + LORA_A:RWKV_COLS]
    gate_ref[...] = _dot(_sigmoid(gd).astype(BF16), g2_ref[...])

    kk = k * kk_ref[...]
    norm = jnp.sqrt(_split_dot(kk * kk, hsum_ref[...]))
    kk = kk / jnp.maximum(norm, 1e-12)
    k2 = k * (1.0 + (alpha - 1.0) * ka_ref[...])
    bonus_ref[...] = _split_dot(r * k2 * rk_ref[...], hsum_ref[...]) * v
    logw_ref[...] = logw
    r_ref[...] = r
    kn_ref[...] = kk
    b_ref[...] = kk * alpha
    k2_ref[...] = k2
    v_ref[...] = v.astype(BF16)

    s_idx = lax.broadcasted_iota(jnp.int32, (L, LANES), 0)
    r_idx = lax.broadcasted_iota(jnp.int32, (L, LANES), 1) % L
    strict = r_idx < s_idx
    incl = r_idx <= s_idx
    eye_p = (r_idx == s_idx).astype(F32)
    d_row = lax.broadcasted_iota(jnp.int32, (LANES, LANES), 0)
    d_col = lax.broadcasted_iota(jnp.int32, (LANES, LANES), 1)
    diag = d_row == d_col
    n_rep = LANES // L
    tril = tril_ref[...]

    def expand(xb):
        return jnp.concatenate([xb] * n_rep, axis=0) * mexp_ref[...]

    def bdot(lhs, yp):
        ybd = jnp.concatenate([yp.astype(BF16)] * n_rep, axis=0) * mbd_ref[...]
        return _dot(lhs.astype(BF16), ybd)

    def stack(top, bottom):
        return jnp.concatenate([top.astype(BF16), bottom.astype(BF16)], axis=0)

    def advance(c, slot, j):
        def step():
            r0 = _scaled(c, L)
            for g in range(N_GROUPS):
                sl = slice(g * LANES, (g + 1) * LANES)
                s2 = state_ref[g].astype(BF16)
                out = _dot(jnp.concatenate([rp_ref[slot, j * L:(j + 1) * L, sl], gm_ref[slot, j, g]],
                                           axis=0), s2)
                y_ref[pl.ds(r0, L), sl] = out[0:L, :] + yi_ref[slot, j * L:(j + 1) * L, sl]
                state_ref[g] = out[L:, :] + hm_ref[slot, j, g]
        return step

    def process(chunks, slot, pending):
        pending = list(pending)

        def tick():
            if pending:
                pending.pop(0)()

        n = range(len(chunks))
        r0 = [_scaled(c, L) for c in chunks]
        at, rt, bt, kt, bh, kh, vv, wl = [], [], [], [], [], [], [], []
        for r in r0:
            lw = logw_ref[pl.ds(r, L), :]
            g_inc = _split_dot_lhs(tril, lw)
            e_in = jnp.exp(g_inc)
            e_neg = jnp.exp(-g_inc)
            e_last = jnp.exp(g_inc[L - 1:L, :] - g_inc)
            kn = kn_ref[pl.ds(r, L), :]
            b = b_ref[pl.ds(r, L), :]
            k2 = k2_ref[pl.ds(r, L), :]
            at.append((-kn * jnp.exp(g_inc - lw)).astype(BF16))
            rt.append(r_ref[pl.ds(r, L), :] * e_in)
            bt.append((b * e_neg).astype(BF16))
            kt.append((k2 * e_neg).astype(BF16))
            bh.append((b * e_last).astype(BF16))
            kh.append((k2 * e_last).astype(BF16))
            vv.append(v_ref[pl.ds(r, L), :])
            wl.append(e_in[L - 1:L, :])
        tick()
        sc = [_dot_nt(stack(at[i], rt[i]),
                      jnp.concatenate([expand(bt[i]), expand(kt[i])], axis=0)) for i in n]
        tick()
        a_ab = [jnp.where(strict, s[0:L, 0:LANES], 0.0) for s in sc]
        a_ak = [jnp.where(strict, s[0:L, LANES:2 * LANES], 0.0) for s in sc]
        m_rb = [jnp.where(incl, s[L:2 * L, 0:LANES], 0.0) for s in sc]
        m_rk = [jnp.where(incl, s[L:2 * L, LANES:2 * LANES], 0.0) for s in sc]
        p = [eye_p + a for a in a_ab]
        apow = [bdot(a, a) for a in a_ab]
        tick()
        for _ in range(2):
            st = [bdot(stack(p[i], apow[i]), apow[i]) for i in n]
            p = [p[i] + st[i][0:L, :] for i in n]
            apow = [s[L:2 * L, :] for s in st]
            tick()
        p = [p[i] + bdot(p[i], apow[i]) for i in n]
        tick()
        mp = [bdot(m_rb[i], p[i]) for i in n]
        tick()
        pm = [stack(p[i], mp[i]) for i in n]
        st = [bdot(pm[i], a_ak[i]) for i in n]
        pq = [stack(st[i][0:L, :], st[i][L:2 * L, :] + m_rk[i]) for i in n]
        tick()
        st = [_dot(pm[i], expand(at[i])) for i in n]
        atp = [s[0:L, :].astype(BF16) for s in st]
        for i in n:
            rp_ref[slot, i * L:(i + 1) * L, :] = (rt[i] + st[i][L:2 * L, :]).astype(BF16)
        tick()
        st = [_dot(pq[i], expand(vv[i])) for i in n]
        uv = [s[0:L, :].astype(BF16) for s in st]
        for i in n:
            yi_ref[slot, i * L:(i + 1) * L, :] = st[i][L:2 * L, :]
        tick()
        for i in n:
            for g in range(N_GROUPS):
                sl = slice(g * LANES, (g + 1) * LANES)
                rhs = jnp.concatenate(
                    [jnp.concatenate([atp[i][:, sl], uv[i][:, sl]], axis=1),
                     jnp.concatenate([jnp.zeros((L, LANES), BF16), vv[i][:, sl]], axis=1)], axis=0)
                out = _dot_tn(jnp.concatenate([bh[i][:, sl], kh[i][:, sl]], axis=0), rhs)
                gm = out[:, 0:LANES] * mpair_ref[...] + jnp.where(diag, wl[i][:, sl], 0.0)
                gm_ref[slot, i, g] = gm.astype(BF16)
                hm_ref[slot, i, g] = out[:, LANES:2 * LANES] * mpair_ref[...]
            tick()
        while pending:
            pending.pop(0)()

    n_chunks = tm // L
    n_full = n_chunks // WKV_GROUP
    rem = n_chunks % WKV_GROUP
    process(list(range(WKV_GROUP)), 0, [])

    def trip(t, carry):
        slot = t % 2
        process([t * WKV_GROUP + u for u in range(WKV_GROUP)], slot,
                [advance((t - 1) * WKV_GROUP + u, 1 - slot, u) for u in range(WKV_GROUP)])
        return carry

    lax.fori_loop(1, n_full, trip, 0)
    last = (n_full - 1) % 2
    process([n_full * WKV_GROUP + u for u in range(rem)], 1 - last,
            [advance((n_full - 1) * WKV_GROUP + u, last, u) for u in range(WKV_GROUP)])
    process([], 1 - last, [advance(n_full * WKV_GROUP + u, 1 - last, u) for u in range(rem)])

    y = y_ref[...]
    mean = _split_dot(y, hsum_ref[...]) * (1.0 / RWKV_HEAD)
    cen = y - mean
    var = _split_dot(cen * cen, hsum_ref[...]) * (1.0 / RWKV_HEAD)
    yn = cen * lax.rsqrt(var + RWKV_GN_EPS) * lng_ref[...] + lnb_ref[...]
    o_ref[0] = ((yn + bonus_ref[...]) * gate_ref[...]).astype(BF16)


def _rwkv(pr, mu, w0, a0, w_lora, g2, k_k, k_a, r_k, lnx_g, lnx_b):
    bsz, t_len, _ = pr.shape
    tm = ROW_TILE
    nt = t_len // tm
    assert (tm // WKV_CHUNK) // WKV_GROUP >= 1 and tm % WKV_CHUNK == 0
    group_rows = WKV_GROUP * WKV_CHUNK

    lane_head = jnp.arange(D_R) // RWKV_HEAD
    hsum = (lane_head[:, None] == lane_head[None, :]).astype(BF16)
    row_head = jnp.arange(LANES) // WKV_CHUNK
    mexp = (row_head[:, None] == lane_head[None, :]).astype(BF16)
    mbd = (row_head[:, None] == row_head[None, :]).astype(BF16)
    pair_head = jnp.arange(LANES) // RWKV_HEAD
    mpair = (pair_head[:, None] == pair_head[None, :]).astype(F32)
    tril = jnp.tril(jnp.ones((WKV_CHUNK, WKV_CHUNK), F32)).astype(BF16)

    tile_f32 = pltpu.VMEM((tm, D_R), F32)
    tile_bf16 = pltpu.VMEM((tm, D_R), BF16)
    return pl.pallas_call(
        _rwkv_kernel,
        out_shape=jax.ShapeDtypeStruct((bsz, t_len, D_R), BF16),
        grid=(bsz, nt),
        in_specs=[
            pl.BlockSpec((1, tm, RWKV_COLS), lambda b, i: (b, i, 0)),
            _const_spec((1, RWKV_COLS)),
            _const_spec((1, D_R)),
            _const_spec((1, D_R)),
            _const_spec((LORA_W + LORA_A, 2 * D_R)),
            _const_spec((LORA_G, D_R)),
            _const_spec((1, D_R)),
            _const_spec((1, D_R)),
            _const_spec((1, D_R)),
            _const_spec((1, D_R)),
            _const_spec((1, D_R)),
            _const_spec((D_R, D_R)),
            _const_spec((LANES, D_R)),
            _const_spec((LANES, LANES)),
            _const_spec((LANES, LANES)),
            _const_spec((WKV_CHUNK, WKV_CHUNK)),
        ],
        out_specs=pl.BlockSpec((1, tm, D_R), lambda b, i: (b, i, 0)),
        scratch_shapes=[
            pltpu.VMEM((1, RWKV_COLS), F32),
            pltpu.VMEM((N_GROUPS, LANES, LANES), F32),
            tile_f32, tile_f32, tile_f32, tile_f32, tile_f32, tile_bf16,
            pltpu.VMEM((2, group_rows, D_R), BF16),
            pltpu.VMEM((2, group_rows, D_R), F32),
            pltpu.VMEM((2, WKV_GROUP, N_GROUPS, LANES, LANES), BF16),
            pltpu.VMEM((2, WKV_GROUP, N_GROUPS, LANES, LANES), F32),
            tile_f32, tile_f32, tile_f32,
        ],
        compiler_params=pltpu.CompilerParams(
            dimension_semantics=("arbitrary", "arbitrary"),
            vmem_limit_bytes=VMEM_LIMIT_BYTES),
        name="rwkv7",
    )(pr, mu, w0, a0, w_lora, g2, k_k, k_a, r_k, lnx_g, lnx_b, hsum, mexp, mbd, mpair, tril)


def _rope_tables(t_len):
    half = HEAD_DIM // 2
    inv = ROPE_THETA ** (-jnp.arange(half, dtype=F32) / half)
    ang = jnp.arange(t_len, dtype=F32)[:, None] * inv[None, :]
    cos = jnp.cos(ang)
    sin = jnp.sin(ang)
    reps = LANES // HEAD_DIM
    cos_t = jnp.tile(jnp.concatenate([cos, cos], axis=1), (1, reps))
    sin_t = jnp.tile(jnp.concatenate([-sin, sin], axis=1), (1, reps))
    return cos_t, sin_t


def kernel(x, meta_tokens, norm_mix, norm_ffn, norm_final, ev_w_in, ev_conv_a, ev_ln_a_g, ev_ln_a_b, ev_conv_b, ev_w_out, od_w_in, od_sinks, od_mu, od_w0, od_w2, od_a0, od_a2, od_g2, od_k_k, od_k_a, od_r_k, od_lnx_g, od_lnx_b, od_w_out, ff_w_up, ff_conv, ff_conv_b, ff_w_down):
    bsz, seq, d = x.shape
    depth = norm_mix.shape[0]
    t_len = N_META + seq
    row = lambda v: v.reshape(1, -1).astype(F32)
    cos_t, sin_t = _rope_tables(t_len)
    h = x
    for i in range(depth):
        j = i // 2
        mixer_out = None
        if i % 2 == 0:
            h = _even_mixer(h, row(norm_mix[i]), ev_w_in[j].astype(BF16),
                            jnp.repeat(ev_conv_a[j], SUBLANES, axis=0),
                            row(ev_ln_a_g[j]), row(ev_ln_a_b[j]), ev_conv_b[j],
                            ev_w_out[j].astype(BF16),
                            meta=meta_tokens.astype(F32) if i == 0 else None)
        else:
            h2d = h.reshape(bsz * t_len, d)
            qkv, pr = _odd_in_proj(h2d, row(norm_mix[i]), od_w_in[j].astype(BF16))
            y_att = _attention(qkv.reshape(bsz, t_len, ATT_COLS), cos_t, sin_t, row(od_sinks[j]))
            zeros = jnp.zeros((LORA_W, D_R), F32)
            w_lora = jnp.concatenate(
                [jnp.concatenate([od_w2[j], zeros], axis=1),
                 jnp.concatenate([zeros, od_a2[j]], axis=1)], axis=0).astype(BF16)
            y_rwkv = _rwkv(pr.reshape(bsz, t_len, RWKV_COLS), row(od_mu[j]), row(od_w0[j]),
                           row(od_a0[j]), w_lora, od_g2[j].astype(BF16), row(od_k_k[j]),
                           row(od_k_a[j]), row(od_r_k[j]), row(od_lnx_g[j]), row(od_lnx_b[j]))
            mixer_out = (y_att, y_rwkv, od_w_out[j].astype(BF16))
        last = i == depth - 1
        h = _ffn(h, row(norm_ffn[i]), ff_w_up[i].astype(BF16), ff_conv[i], row(ff_conv_b[i]),
                 ff_w_down[i].astype(BF16), row(norm_final), final_norm=last,
                 mixer_out=mixer_out, drop_meta=last)
    return h
```

```python
import functools

import jax
import jax.numpy as jnp
from jax import lax
from jax.experimental import pallas as pl
from jax.experimental.pallas import tpu as pltpu

D_MODEL = 1024
N_META = 16
RMS_EPS = 1e-6
LN_EPS = 1e-5
D_A = 512
D_B = 512
CONV_A_WIDTH = 31
CONV_B_WIDTH = 3
HEAD_DIM = 64
N_Q_HEADS = 8
N_KV_HEADS = 2
GQA_GROUP = N_Q_HEADS // N_KV_HEADS
D_ATT = N_Q_HEADS * HEAD_DIM
D_KV = N_KV_HEADS * HEAD_DIM
WINDOW = 128
BLOCK = 128
ROPE_THETA = 10000.0
D_R = 512
N_R_HEADS = 8
RWKV_HEAD = 64
LORA_W = 64
LORA_A = 64
LORA_G = 128
RWKV_GN_EPS = 64e-5
ATT_COLS = D_ATT + 2 * D_KV
RWKV_COLS = 3 * D_R + LORA_W + LORA_A + LORA_G
D_FF = 2816
NEG_INF = -1e30

VMEM_LIMIT_BYTES = 56 * 1024 * 1024
LANES = 128
SUBLANES = 8
ROW_TILE = 688
FINAL_ROW_TILE = 512
CONV_ROWS = 16
CONV_GROUP = 4
FF_CHUNK = 256
WKV_CHUNK = 16
WKV_GROUP = 16
HEADS_PER_GROUP = LANES // RWKV_HEAD
N_GROUPS = N_R_HEADS // HEADS_PER_GROUP

F32 = jnp.float32
BF16 = jnp.bfloat16


def _const_spec(shape):
    nd = len(shape)
    return pl.BlockSpec(shape, lambda *_: (0,) * nd, pipeline_mode=pl.Buffered(1))


def _layer_spec(shape, layer):
    nd = len(shape)
    return pl.BlockSpec((None,) + tuple(shape), lambda *_: (layer,) + (0,) * nd,
                        pipeline_mode=pl.Buffered(1))


def _rms_norm(x, g):
    ms = jnp.mean(x * x, axis=-1, keepdims=True)
    return x * lax.rsqrt(ms + RMS_EPS) * g


def _sigmoid(x):
    return 1.0 / (1.0 + jnp.exp(-x))


def _dot(a, b):
    return jnp.dot(a, b, preferred_element_type=F32)


def _dot_nt(a, b):
    return lax.dot_general(a, b, (((1,), (1,)), ((), ())), preferred_element_type=F32)


def _dot_tn(a, b):
    return lax.dot_general(a, b, (((0,), (0,)), ((), ())), preferred_element_type=F32)


def _scaled(i, m):
    return i * m if isinstance(i, int) else pl.multiple_of(i * m, m)


def _grouped_loop(n, group, body):
    def trip(t, carry):
        body([t * group + u for u in range(group)])
        return carry

    lax.fori_loop(0, n // group, trip, 0)
    if n % group:
        body(list(range(n - n % group, n)))


def _split_dot(x, w_bf16):
    hi = x.astype(BF16)
    lo = (x - hi.astype(F32)).astype(BF16)
    return _dot(hi, w_bf16) + _dot(lo, w_bf16)


def _split_dot_lhs(w_bf16, x):
    hi = x.astype(BF16)
    lo = (x - hi.astype(F32)).astype(BF16)
    return _dot(w_bf16, hi) + _dot(w_bf16, lo)


A_HIST = 32
B_HIST = 8


def _even_mixer_kernel(*refs, prepend_meta):
    if prepend_meta:
        h_ref, meta_ref = refs[:2]
        refs = refs[1:]
    else:
        h_ref = refs[0]
    _, g_ref, win_ref, ca_ref, lng_ref, lnb_ref, cb_ref, wout_ref, o_ref = refs[:9]
    ua_ref, ush_ref, ub_ref, y_ref = refs[9:13]
    tm = h_ref.shape[1]
    i = pl.program_id(1)

    @pl.when(i == 0)
    def _():
        ua_ref[0:A_HIST, :] = jnp.zeros((A_HIST, D_A), F32)
        ub_ref[0:B_HIST, :] = jnp.zeros((B_HIST, D_B), F32)
        ua_ref[A_HIST + tm:A_HIST + tm + SUBLANES, :] = jnp.zeros((SUBLANES, D_A), F32)

    @pl.when(i > 0)
    def _():
        ua_ref[0:A_HIST, :] = ua_ref[tm:tm + A_HIST, :]
        ub_ref[0:B_HIST, :] = ub_ref[tm:tm + B_HIST, :]

    if prepend_meta:
        xt_ref = refs[13]

        @pl.when(i == 0)
        def _():
            xt_ref[0:N_META, :] = meta_ref[...]
            xt_ref[N_META:tm, :] = h_ref[0, 0:tm - N_META, :]

        @pl.when(i > 0)
        def _():
            xt_ref[...] = h_ref[0]

        x = xt_ref[...]
    else:
        x = h_ref[0]
    hn = _rms_norm(x, g_ref[...]).astype(BF16)
    a_val = _dot(hn, win_ref[:, 0:D_A])
    a_gate = _dot(hn, win_ref[:, D_A:2 * D_A])
    ua_ref[A_HIST:A_HIST + tm, :] = a_val * _sigmoid(a_gate)
    g_c = _dot(hn, win_ref[:, 2 * D_A + D_B:2 * D_A + 2 * D_B])
    x_in = _dot(hn, win_ref[:, 2 * D_A + 2 * D_B:2 * D_A + 3 * D_B])
    ub_ref[B_HIST:B_HIST + tm, :] = g_c * x_in

    conv_b = jnp.zeros((tm, D_B), F32)
    for j in range(CONV_B_WIDTH):
        off = B_HIST - (CONV_B_WIDTH - 1) + j
        conv_b = conv_b + cb_ref[j:j + 1, :] * ub_ref[off:off + tm, :]
    g_b = _dot(hn, win_ref[:, 2 * D_A:2 * D_A + D_B])
    y_ref[:, D_A:D_A + D_B] = (g_b * conv_b).astype(BF16)

    n_sh = ush_ref.shape[1]
    for s in range(SUBLANES):
        ush_ref[s] = ua_ref[s:s + n_sh, :]

    lng = lng_ref[...]
    lnb = lnb_ref[...]
    first = A_HIST - (CONV_A_WIDTH - 1)

    def conv_blocks(blocks):
        starts = [_scaled(ci, CONV_ROWS) for ci in blocks]
        accs = [jnp.zeros((CONV_ROWS, D_A), F32) for _ in blocks]
        for j in range(CONV_A_WIDTH):
            q, s = divmod(first + j, SUBLANES)
            w8 = ca_ref[j * SUBLANES:(j + 1) * SUBLANES, :]
            w = jnp.concatenate([w8] * (CONV_ROWS // SUBLANES), axis=0)
            for u, r0 in enumerate(starts):
                accs[u] = accs[u] + w * ush_ref[s, pl.ds(r0 + q * SUBLANES, CONV_ROWS), :]
        for acc, r0 in zip(accs, starts):
            mu = jnp.mean(acc, axis=-1, keepdims=True)
            cen = acc - mu
            var = jnp.mean(cen * cen, axis=-1, keepdims=True)
            ya = cen * lax.rsqrt(var + LN_EPS) * lng + lnb
            ya = ya * _sigmoid(ya)
            y_ref[pl.ds(r0, CONV_ROWS), 0:D_A] = ya.astype(BF16)

    n_blocks = tm // CONV_ROWS
    for b0 in range(0, n_blocks, CONV_GROUP):
        conv_blocks(list(range(b0, min(b0 + CONV_GROUP, n_blocks))))
    o_ref[0] = x + _dot(y_ref[...], wout_ref[...])


def _even_mixer(h, g, w_in, conv_a, ln_g, ln_b, conv_b, w_out, meta=None):
    bsz, rows_in, d = h.shape
    tm = ROW_TILE
    n_in = w_in.shape[1]
    scratch = [
        pltpu.VMEM((A_HIST + tm + SUBLANES, D_A), F32),
        pltpu.VMEM((SUBLANES, A_HIST + tm, D_A), F32),
        pltpu.VMEM((B_HIST + tm, D_B), F32),
        pltpu.VMEM((tm, D_A + D_B), BF16),
    ]
    if meta is None:
        t_len = rows_in
        h_specs = [pl.BlockSpec((1, tm, d), lambda b, i: (b, i, 0))]
        h_args = (h,)
    else:
        t_len = rows_in + N_META
        h_specs = [pl.BlockSpec((pl.Element(1), pl.Element(tm), pl.Element(d)),
                                lambda b, i: (b, pl.multiple_of(
                                    jnp.maximum(i * tm - N_META, 0), SUBLANES), 0)),
                   _const_spec((N_META, d))]
        h_args = (h, meta)
        scratch.append(pltpu.VMEM((tm, d), F32))
    nt = t_len // tm
    return pl.pallas_call(
        functools.partial(_even_mixer_kernel, prepend_meta=meta is not None),
        out_shape=jax.ShapeDtypeStruct((bsz, t_len, d), F32),
        grid=(bsz, nt),
        in_specs=[
            *h_specs,
            _const_spec((1, d)),
            _const_spec((d, n_in)),
            _const_spec((CONV_A_WIDTH * SUBLANES, D_A)),
            _const_spec((1, D_A)),
            _const_spec((1, D_A)),
            _const_spec((CONV_B_WIDTH, D_B)),
            _const_spec((D_A + D_B, d)),
        ],
        out_specs=pl.BlockSpec((1, tm, d), lambda b, i: (b, i, 0)),
        scratch_shapes=scratch,
        compiler_params=pltpu.CompilerParams(
            dimension_semantics=("arbitrary", "arbitrary"),
            vmem_limit_bytes=VMEM_LIMIT_BYTES),
        name="even_mixer",
    )(*h_args, g, w_in, conv_a, ln_g, ln_b, conv_b, w_out)


F_HIST = 8


def _ffn_kernel(*refs, final_norm, mixer_out, halo):
    if mixer_out:
        h_ref, ya_ref, yr_ref, wo_ref = refs[:4]
        refs = refs[3:]
    else:
        h_ref = refs[0]
    _, g_ref, wup_ref, cw_ref, cb_ref, wdn_ref, gf_ref, o_ref = refs[:8]
    if halo:
        (act_ref,) = refs[8:]
    else:
        carry_ref, act_ref = refs[8:]
    tm = h_ref.shape[1]
    i = pl.program_id(1)
    n_chunks = D_FF // FF_CHUNK

    if not halo:
        @pl.when(i == 0)
        def _():
            carry_ref[...] = jnp.zeros(carry_ref.shape, F32)

    x = h_ref[0]
    if mixer_out:
        x = (x + _dot(ya_ref[0], wo_ref[0:D_ATT, :]) + _dot(yr_ref[0], wo_ref[D_ATT:D_ATT + D_R, :]))
    hn = _rms_norm(x, g_ref[...]).astype(BF16)

    def up_proj(c):
        c0 = c * FF_CHUNK
        return (_dot(hn, wup_ref[:, c0:c0 + FF_CHUNK]),
                _dot(hn, wup_ref[:, D_FF + c0:D_FF + c0 + FF_CHUNK]))

    nxt = up_proj(0)
    for c in range(n_chunks):
        g_raw, val = nxt
        if c + 1 < n_chunks:
            nxt = up_proj(c + 1)
        cols = slice(c * FF_CHUNK, (c + 1) * FF_CHUNK)
        if halo:
            prev1 = pltpu.roll(g_raw, 1, axis=0)
            prev2 = pltpu.roll(g_raw, 2, axis=0)
        else:
            ext = jnp.concatenate([carry_ref[:, cols], g_raw], axis=0)
            carry_ref[:, cols] = g_raw[tm - F_HIST:tm, :]
            prev1 = pltpu.roll(ext, 1, axis=0)[F_HIST:, :]
            prev2 = pltpu.roll(ext, 2, axis=0)[F_HIST:, :]
        gate = (cb_ref[:, cols] + cw_ref[2:3, cols] * g_raw
                + cw_ref[1:2, cols] * prev1 + cw_ref[0:1, cols] * prev2)
        act_ref[:, cols] = (gate * _sigmoid(gate) * val).astype(BF16)

    out = x + _dot(act_ref[...], wdn_ref[...])
    if final_norm:
        out = _rms_norm(out, gf_ref[...])
    o_ref[0] = out[halo:, :]


def _ffn(h, g, w_up, conv_w, conv_b, w_down, g_final, layer, final_norm, mixer_out=None,
         drop_meta=False):
    bsz, t_len, d = h.shape
    if drop_meta:
        halo = N_META
        out_rows = FINAL_ROW_TILE
        nt = (t_len - N_META) // out_rows
        tm = out_rows + halo

        def rows_spec(width):
            return pl.BlockSpec((pl.Element(1), pl.Element(tm), pl.Element(width)),
                                lambda b, i: (b, i * out_rows, 0))
        out_len = t_len - N_META
    else:
        halo = 0
        out_rows = tm = ROW_TILE
        nt = t_len // tm

        def rows_spec(width):
            return pl.BlockSpec((1, tm, width), lambda b, i: (b, i, 0))
        out_len = t_len
    mix_args, mix_specs = (), []
    if mixer_out is not None:
        y_att, y_rwkv, w_out = mixer_out
        mix_args = (y_att, y_rwkv, w_out)
        mix_specs = [rows_spec(D_ATT), rows_spec(D_R), _const_spec((D_ATT + D_R, d))]
    scratch = [] if halo else [pltpu.VMEM((F_HIST, D_FF), F32)]
    return pl.pallas_call(
        functools.partial(_ffn_kernel, final_norm=final_norm, mixer_out=mixer_out is not None,
                          halo=halo),
        out_shape=jax.ShapeDtypeStruct((bsz, out_len, d), F32),
        grid=(bsz, nt),
        in_specs=[
            rows_spec(d),
            *mix_specs,
            _const_spec((1, d)),
            _layer_spec((d, 2 * D_FF), layer),
            _layer_spec((3, D_FF), layer),
            _layer_spec((1, D_FF), layer),
            _layer_spec((D_FF, d), layer),
            _const_spec((1, d)),
        ],
        out_specs=pl.BlockSpec((1, out_rows, d), lambda b, i: (b, i, 0)),
        scratch_shapes=scratch + [pltpu.VMEM((tm, D_FF), BF16)],
        compiler_params=pltpu.CompilerParams(
            dimension_semantics=("arbitrary", "arbitrary"),
            vmem_limit_bytes=VMEM_LIMIT_BYTES),
        name="ffn_final" if final_norm else "ffn",
    )(h, *mix_args, g, w_up, conv_w, conv_b, w_down, g_final)


def _odd_in_kernel(h_ref, g_ref, w_ref, qkv_ref, pr_ref):
    hn = _rms_norm(h_ref[...], g_ref[...]).astype(BF16)
    qkv_ref[...] = _dot(hn, w_ref[:, 0:ATT_COLS])
    pr_ref[...] = _dot(hn, w_ref[:, ATT_COLS:ATT_COLS + RWKV_COLS])


def _odd_in_proj(h2d, g, w_in):
    rows, d = h2d.shape
    tm = ROW_TILE
    return pl.pallas_call(
        _odd_in_kernel,
        out_shape=(jax.ShapeDtypeStruct((rows, ATT_COLS), F32),
                   jax.ShapeDtypeStruct((rows, RWKV_COLS), F32)),
        grid=(rows // tm,),
        in_specs=[
            pl.BlockSpec((tm, d), lambda i: (i, 0)),
            _const_spec((1, d)),
            _const_spec((d, ATT_COLS + RWKV_COLS)),
        ],
        out_specs=(pl.BlockSpec((tm, ATT_COLS), lambda i: (i, 0)),
                   pl.BlockSpec((tm, RWKV_COLS), lambda i: (i, 0))),
        compiler_params=pltpu.CompilerParams(
            dimension_semantics=("arbitrary",),
            vmem_limit_bytes=VMEM_LIMIT_BYTES),
        name="odd_in_proj",
    )(h2d, g, w_in)


def _rope(x, cos, sin):
    lane = lax.broadcasted_iota(jnp.int32, x.shape, 1)
    first_half = (lane % HEAD_DIM) < (HEAD_DIM // 2)
    rot = jnp.where(first_half,
                    pltpu.roll(x, LANES - HEAD_DIM // 2, axis=1),
                    pltpu.roll(x, HEAD_DIM // 2, axis=1))
    return x * cos + rot * sin


K_PAD = BLOCK - N_META
N_EXTRA = 2 * N_META
N_KEYS = 2 * BLOCK + N_EXTRA
SINK_COL = 2 * BLOCK + N_META
Q_ROWS = GQA_GROUP * BLOCK
ATTN_GROUP = 4
V_WIDTH = 2 * LANES


def _attn_kernel(qkv_ref, cos_ref, sin_ref, bias_ref, mbias_ref, o_ref,
                 q_ref, k_ref, v_ref, kx_ref, vx_ref):
    t_len = qkv_ref.shape[1]
    scale = HEAD_DIM ** -0.5
    rt = ROW_TILE
    heads_per_group = LANES // HEAD_DIM

    k_ref[:, 0:K_PAD, :] = jnp.zeros((N_KV_HEADS, K_PAD, HEAD_DIM), BF16)
    v_ref[:, 0:K_PAD, :] = jnp.zeros((N_KV_HEADS, K_PAD, V_WIDTH), BF16)

    def rope_rows(ti, carry):
        r0 = pl.multiple_of(ti * rt, 16)
        cos = cos_ref[pl.ds(r0, rt), :]
        sin = sin_ref[pl.ds(r0, rt), :]
        for gq in range(D_ATT // LANES):
            xq = qkv_ref[0, pl.ds(r0, rt), gq * LANES:(gq + 1) * LANES]
            xq = (_rope(xq, cos, sin) * scale).astype(BF16)
            for u in range(heads_per_group):
                q_ref[gq * heads_per_group + u, pl.ds(r0, rt), :] = xq[:, u * HEAD_DIM:(u + 1) * HEAD_DIM]
        xk = _rope(qkv_ref[0, pl.ds(r0, rt), D_ATT:D_ATT + D_KV], cos, sin).astype(BF16)
        xv = qkv_ref[0, pl.ds(r0, rt), D_ATT + D_KV:ATT_COLS].astype(BF16)
        ones = jnp.ones((rt, LANES), BF16)
        for u in range(N_KV_HEADS):
            k_ref[u, pl.ds(K_PAD + r0, rt), :] = xk[:, u * HEAD_DIM:(u + 1) * HEAD_DIM]
            vu = xv[:, u * HEAD_DIM:(u + 1) * HEAD_DIM]
            v_ref[u, pl.ds(K_PAD + r0, rt), :] = jnp.concatenate([vu, vu, ones], axis=1)
        return carry

    lax.fori_loop(0, t_len // rt, rope_rows, 0)

    kx_ref[:, 0:N_META, :] = k_ref[:, K_PAD:K_PAD + N_META, :]
    kx_ref[:, N_META:N_EXTRA, :] = jnp.zeros((N_KV_HEADS, N_EXTRA - N_META, HEAD_DIM), BF16)
    vx_ref[:, 0:N_META, :] = v_ref[:, K_PAD:K_PAD + N_META, :]
    vx_ref[:, N_META:N_EXTRA, :] = jnp.concatenate(
        [jnp.zeros((N_KV_HEADS, N_EXTRA - N_META, LANES), BF16),
         jnp.ones((N_KV_HEADS, N_EXTRA - N_META, LANES), BF16)], axis=2)

    def softmax_pv(items, n_q):
        s = []
        for g, q_start, keys, _, bias in items:
            q4 = q_ref[g * GQA_GROUP:(g + 1) * GQA_GROUP, pl.ds(q_start, n_q), :]
            s.append(_dot_nt(q4.reshape(GQA_GROUP * n_q, HEAD_DIM), keys) + bias)
        p = [jnp.exp(x - jnp.max(x, axis=-1, keepdims=True)).astype(BF16) for x in s]
        r = [_dot(p[i], items[i][3]) for i in range(len(items))]
        lane = lax.broadcasted_iota(jnp.int32, (n_q, LANES), 1)
        for (g, q_start, _, _, _), ri in zip(items, r):
            o = ri[:, 0:LANES] / ri[:, LANES:V_WIDTH]
            for u in range(GQA_GROUP // heads_per_group):
                pair = jnp.where(lane < HEAD_DIM, o[2 * u * n_q:(2 * u + 1) * n_q, :],
                                 o[(2 * u + 1) * n_q:(2 * u + 2) * n_q, :])
                col = (g * GQA_GROUP + 2 * u) * HEAD_DIM
                o_ref[0, pl.ds(q_start, n_q), col:col + LANES] = pair.astype(BF16)

    softmax_pv([(g, 0, kx_ref[g], vx_ref[g], mbias_ref[g]) for g in range(N_KV_HEADS)], N_META)

    def blocks(ns):
        items = []
        for n in ns:
            band = _scaled(n, BLOCK)
            variant = min(n, 1) if isinstance(n, int) else jnp.minimum(n, 1)
            for g in range(N_KV_HEADS):
                keys = jnp.concatenate([k_ref[g, pl.ds(band, 2 * BLOCK), :], kx_ref[g]], axis=0)
                vals = jnp.concatenate([v_ref[g, pl.ds(band, 2 * BLOCK), :], vx_ref[g]], axis=0)
                items.append((g, N_META + band, keys, vals, bias_ref[variant, g]))
        softmax_pv(items, BLOCK)

    _grouped_loop((t_len - N_META) // BLOCK, ATTN_GROUP, blocks)


def _attention(qkv, cos, sin, sinks):
    bsz, t_len, _ = qkv.shape
    def stacked_bias(n_q, visible):
        sink = sinks.reshape(N_KV_HEADS, GQA_GROUP, 1, 1).astype(F32)
        base = jnp.where(visible, 0.0, NEG_INF).astype(F32)
        col = jnp.arange(visible.shape[1])[None, :]
        tiled = jnp.broadcast_to(base[None, None], (N_KV_HEADS, GQA_GROUP) + base.shape)
        tiled = jnp.where(col == visible.shape[1] - N_EXTRA + N_META, sink, tiled)
        return tiled.reshape(N_KV_HEADS, GQA_GROUP * n_q, visible.shape[1])

    qi = jnp.arange(BLOCK)[:, None]
    kj = jnp.arange(N_KEYS)[None, :]
    in_window = (kj >= qi + 1) & (kj <= qi + WINDOW) & (kj < 2 * BLOCK)
    is_meta = (kj >= 2 * BLOCK) & (kj < 2 * BLOCK + N_META)
    bias = jnp.stack([stacked_bias(BLOCK, (in_window & (kj >= BLOCK)) | is_meta),
                      stacked_bias(BLOCK, in_window | is_meta)])
    mi = jnp.arange(N_META)[:, None]
    mj = jnp.arange(N_EXTRA)[None, :]
    mbias = stacked_bias(N_META, mj <= mi)
    return pl.pallas_call(
        _attn_kernel,
        out_shape=jax.ShapeDtypeStruct((bsz, t_len, D_ATT), BF16),
        grid=(bsz,),
        in_specs=[
            pl.BlockSpec((1, t_len, ATT_COLS), lambda b: (b, 0, 0)),
            _const_spec((t_len, LANES)),
            _const_spec((t_len, LANES)),
            _const_spec((2, N_KV_HEADS, Q_ROWS, N_KEYS)),
            _const_spec((N_KV_HEADS, GQA_GROUP * N_META, N_EXTRA)),
        ],
        out_specs=pl.BlockSpec((1, t_len, D_ATT), lambda b: (b, 0, 0)),
        scratch_shapes=[
            pltpu.VMEM((N_Q_HEADS, t_len, HEAD_DIM), BF16),
            pltpu.VMEM((N_KV_HEADS, K_PAD + t_len, HEAD_DIM), BF16),
            pltpu.VMEM((N_KV_HEADS, K_PAD + t_len, V_WIDTH), BF16),
            pltpu.VMEM((N_KV_HEADS, N_EXTRA, HEAD_DIM), BF16),
            pltpu.VMEM((N_KV_HEADS, N_EXTRA, V_WIDTH), BF16),
        ],
        compiler_params=pltpu.CompilerParams(
            dimension_semantics=("arbitrary",),
            vmem_limit_bytes=VMEM_LIMIT_BYTES),
        name="swa_attention",
    )(qkv, cos, sin, bias, mbias)


def _rwkv_kernel(pr_ref, mu_ref, w0_ref, a0_ref, wlora_ref, g2_ref, kk_ref, ka_ref, rk_ref,
                 lng_ref, lnb_ref, hsum_ref, mexp_ref, mbd_ref, mpair_ref, tril_ref,
                 o_ref,
                 prev_ref, state_ref, logw_ref, r_ref, kn_ref, b_ref, k2_ref, v_ref,
                 rp_ref, yi_ref, gm_ref, hm_ref, y_ref, bonus_ref, gate_ref):
    tm = pr_ref.shape[1]
    i = pl.program_id(1)
    L = WKV_CHUNK

    @pl.when(i == 0)
    def _():
        prev_ref[...] = jnp.zeros(prev_ref.shape, F32)
        state_ref[...] = jnp.zeros(state_ref.shape, F32)

    x = pr_ref[0]
    row = lax.broadcasted_iota(jnp.int32, (tm, 1), 0)
    prev = jnp.where(row == 0, prev_ref[...], pltpu.roll(x, 1, axis=0))
    prev_ref[...] = x[tm - 1:tm, :]
    xm = x + (prev - x) * mu_ref[...]
    r = xm[:, 0:D_R]
    k = xm[:, D_R:2 * D_R]
    v = xm[:, 2 * D_R:3 * D_R]
    lora_in = xm[:, 3 * D_R:3 * D_R + LORA_W + LORA_A]
    lane = lax.broadcasted_iota(jnp.int32, lora_in.shape, 1)
    lora_in = jnp.where(lane < LORA_W, jnp.tanh(lora_in), lora_in).astype(BF16)
    lora = _dot(lora_in, wlora_ref[...])
    z = -(w0_ref[...] + lora[:, 0:D_R])
    softplus = jnp.maximum(z, 0.0) + jnp.log(1.0 + jnp.exp(-jnp.abs(z)))
    logw = -jnp.exp(-softplus - 0.5)
    alpha = _sigmoid(a0_ref[...] + lora[:, D_R:2 * D_R])
    gd = xm[:, 3 * D_R + LORA_W + LORA_A:RWKV_COLS]
    gate_ref[...] = _dot(_sigmoid(gd).astype(BF16), g2_ref[...])

    kk = k * kk_ref[...]
    norm = jnp.sqrt(_split_dot(kk * kk, hsum_ref[...]))
    kk = kk / jnp.maximum(norm, 1e-12)
    k2 = k * (1.0 + (alpha - 1.0) * ka_ref[...])
    bonus_ref[...] = _split_dot(r * k2 * rk_ref[...], hsum_ref[...]) * v
    logw_ref[...] = logw
    r_ref[...] = r
    kn_ref[...] = kk
    b_ref[...] = kk * alpha
    k2_ref[...] = k2
    v_ref[...] = v.astype(BF16)

    s_idx = lax.broadcasted_iota(jnp.int32, (L, LANES), 0)
    r_idx = lax.broadcasted_iota(jnp.int32, (L, LANES), 1) % L
    strict = r_idx < s_idx
    incl = r_idx <= s_idx
    eye_p = (r_idx == s_idx).astype(F32)
    d_row = lax.broadcasted_iota(jnp.int32, (LANES, LANES), 0)
    d_col = lax.broadcasted_iota(jnp.int32, (LANES, LANES), 1)
    diag = d_row == d_col
    n_rep = LANES // L
    tril = tril_ref[...]

    def expand(xb):
        return jnp.concatenate([xb] * n_rep, axis=0) * mexp_ref[...]

    def bdot(lhs, yp):
        ybd = jnp.concatenate([yp.astype(BF16)] * n_rep, axis=0) * mbd_ref[...]
        return _dot(lhs.astype(BF16), ybd)

    def stack(top, bottom):
        return jnp.concatenate([top.astype(BF16), bottom.astype(BF16)], axis=0)

    def advance(c, slot, j):
        def step():
            r0 = _scaled(c, L)
            for g in range(N_GROUPS):
                sl = slice(g * LANES, (g + 1) * LANES)
                s2 = state_ref[g].astype(BF16)
                out = _dot(jnp.concatenate([rp_ref[slot, j * L:(j + 1) * L, sl], gm_ref[slot, j, g]],
                                           axis=0), s2)
                y_ref[pl.ds(r0, L), sl] = out[0:L, :] + yi_ref[slot, j * L:(j + 1) * L, sl]
                state_ref[g] = out[L:, :] + hm_ref[slot, j, g]
        return step

    def process(chunks, slot, pending):
        pending = list(pending)

        def tick():
            if pending:
                pending.pop(0)()

        n = range(len(chunks))
        r0 = [_scaled(c, L) for c in chunks]
        at, rt, bt, kt, bh, kh, vv, wl = [], [], [], [], [], [], [], []
        for r in r0:
            lw = logw_ref[pl.ds(r, L), :]
            g_inc = _split_dot_lhs(tril, lw)
            e_in = jnp.exp(g_inc)
            e_neg = jnp.exp(-g_inc)
            e_last = jnp.exp(g_inc[L - 1:L, :] - g_inc)
            kn = kn_ref[pl.ds(r, L), :]
            b = b_ref[pl.ds(r, L), :]
            k2 = k2_ref[pl.ds(r, L), :]
            at.append((-kn * jnp.exp(g_inc - lw)).astype(BF16))
            rt.append(r_ref[pl.ds(r, L), :] * e_in)
            bt.append((b * e_neg).astype(BF16))
            kt.append((k2 * e_neg).astype(BF16))
            bh.append((b * e_last).astype(BF16))
            kh.append((k2 * e_last).astype(BF16))
            vv.append(v_ref[pl.ds(r, L), :])
            wl.append(e_in[L - 1:L, :])
        tick()
        sc = [_dot_nt(stack(at[i], rt[i]),
                      jnp.concatenate([expand(bt[i]), expand(kt[i])], axis=0)) for i in n]
        tick()
        a_ab = [jnp.where(strict, s[0:L, 0:LANES], 0.0) for s in sc]
        a_ak = [jnp.where(strict, s[0:L, LANES:2 * LANES], 0.0) for s in sc]
        m_rb = [jnp.where(incl, s[L:2 * L, 0:LANES], 0.0) for s in sc]
        m_rk = [jnp.where(incl, s[L:2 * L, LANES:2 * LANES], 0.0) for s in sc]
        p = [eye_p + a for a in a_ab]
        apow = [bdot(a, a) for a in a_ab]
        tick()
        for _ in range(2):
            st = [bdot(stack(p[i], apow[i]), apow[i]) for i in n]
            p = [p[i] + st[i][0:L, :] for i in n]
            apow = [s[L:2 * L, :] for s in st]
            tick()
        p = [p[i] + bdot(p[i], apow[i]) for i in n]
        tick()
        mp = [bdot(m_rb[i], p[i]) for i in n]
        tick()
        pm = [stack(p[i], mp[i]) for i in n]
        st = [bdot(pm[i], a_ak[i]) for i in n]
        pq = [stack(st[i][0:L, :], st[i][L:2 * L, :] + m_rk[i]) for i in n]
        tick()
        st = [_dot(pm[i], expand(at[i])) for i in n]
        atp = [s[0:L, :].astype(BF16) for s in st]
        for i in n:
            rp_ref[slot, i * L:(i + 1) * L, :] = (rt[i] + st[i][L:2 * L, :]).astype(BF16)
        tick()
        st = [_dot(pq[i], expand(vv[i])) for i in n]
        uv = [s[0:L, :].astype(BF16) for s in st]
        for i in n:
            yi_ref[slot, i * L:(i + 1) * L, :] = st[i][L:2 * L, :]
        tick()
        for i in n:
            for g in range(N_GROUPS):
                sl = slice(g * LANES, (g + 1) * LANES)
                rhs = jnp.concatenate(
                    [jnp.concatenate([atp[i][:, sl], uv[i][:, sl]], axis=1),
                     jnp.concatenate([jnp.zeros((L, LANES), BF16), vv[i][:, sl]], axis=1)], axis=0)
                out = _dot_tn(jnp.concatenate([bh[i][:, sl], kh[i][:, sl]], axis=0), rhs)
                gm = out[:, 0:LANES] * mpair_ref[...] + jnp.where(diag, wl[i][:, sl], 0.0)
                gm_ref[slot, i, g] = gm.astype(BF16)
                hm_ref[slot, i, g] = out[:, LANES:2 * LANES] * mpair_ref[...]
            tick()
        while pending:
            pending.pop(0)()

    n_chunks = tm // L
    n_full = n_chunks // WKV_GROUP
    rem = n_chunks % WKV_GROUP
    process(list(range(WKV_GROUP)), 0, [])

    def trip(t, carry):
        slot = t % 2
        process([t * WKV_GROUP + u for u in range(WKV_GROUP)], slot,
                [advance((t - 1) * WKV_GROUP + u, 1 - slot, u) for u in range(WKV_GROUP)])
        return carry

    lax.fori_loop(1, n_full, trip, 0)
    last = (n_full - 1) % 2
    process([n_full * WKV_GROUP + u for u in range(rem)], 1 - last,
            [advance((n_full - 1) * WKV_GROUP + u, last, u) for u in range(WKV_GROUP)])
    process([], 1 - last, [advance(n_full * WKV_GROUP + u, 1 - last, u) for u in range(rem)])

    y = y_ref[...]
    mean = _split_dot(y, hsum_ref[...]) * (1.0 / RWKV_HEAD)
    cen = y - mean
    var = _split_dot(cen * cen, hsum_ref[...]) * (1.0 / RWKV_HEAD)
    yn = cen * lax.rsqrt(var + RWKV_GN_EPS) * lng_ref[...] + lnb_ref[...]
    o_ref[0] = ((yn + bonus_ref[...]) * gate_ref[...]).astype(BF16)


def _rwkv(pr, mu, w0, a0, w_lora, g2, k_k, k_a, r_k, lnx_g, lnx_b):
    bsz, t_len, _ = pr.shape
    tm = ROW_TILE
    nt = t_len // tm
    assert (tm // WKV_CHUNK) // WKV_GROUP >= 1 and tm % WKV_CHUNK == 0
    group_rows = WKV_GROUP * WKV_CHUNK

    lane_head = jnp.arange(D_R) // RWKV_HEAD
    hsum = (lane_head[:, None] == lane_head[None, :]).astype(BF16)
    row_head = jnp.arange(LANES) // WKV_CHUNK
    mexp = (row_head[:, None] == lane_head[None, :]).astype(BF16)
    mbd = (row_head[:, None] == row_head[None, :]).astype(BF16)
    pair_head = jnp.arange(LANES) // RWKV_HEAD
    mpair = (pair_head[:, None] == pair_head[None, :]).astype(F32)
    tril = jnp.tril(jnp.ones((WKV_CHUNK, WKV_CHUNK), F32)).astype(BF16)

    tile_f32 = pltpu.VMEM((tm, D_R), F32)
    tile_bf16 = pltpu.VMEM((tm, D_R), BF16)
    return pl.pallas_call(
        _rwkv_kernel,
        out_shape=jax.ShapeDtypeStruct((bsz, t_len, D_R), BF16),
        grid=(bsz, nt),
        in_specs=[
            pl.BlockSpec((1, tm, RWKV_COLS), lambda b, i: (b, i, 0)),
            _const_spec((1, RWKV_COLS)),
            _const_spec((1, D_R)),
            _const_spec((1, D_R)),
            _const_spec((LORA_W + LORA_A, 2 * D_R)),
            _const_spec((LORA_G, D_R)),
            _const_spec((1, D_R)),
            _const_spec((1, D_R)),
            _const_spec((1, D_R)),
            _const_spec((1, D_R)),
            _const_spec((1, D_R)),
            _const_spec((D_R, D_R)),
            _const_spec((LANES, D_R)),
            _const_spec((LANES, LANES)),
            _const_spec((LANES, LANES)),
            _const_spec((WKV_CHUNK, WKV_CHUNK)),
        ],
        out_specs=pl.BlockSpec((1, tm, D_R), lambda b, i: (b, i, 0)),
        scratch_shapes=[
            pltpu.VMEM((1, RWKV_COLS), F32),
            pltpu.VMEM((N_GROUPS, LANES, LANES), F32),
            tile_f32, tile_f32, tile_f32, tile_f32, tile_f32, tile_bf16,
            pltpu.VMEM((2, group_rows, D_R), BF16),
            pltpu.VMEM((2, group_rows, D_R), F32),
            pltpu.VMEM((2, WKV_GROUP, N_GROUPS, LANES, LANES), BF16),
            pltpu.VMEM((2, WKV_GROUP, N_GROUPS, LANES, LANES), F32),
            tile_f32, tile_f32, tile_f32,
        ],
        compiler_params=pltpu.CompilerParams(
            dimension_semantics=("arbitrary", "arbitrary"),
            vmem_limit_bytes=VMEM_LIMIT_BYTES),
        name="rwkv7",
    )(pr, mu, w0, a0, w_lora, g2, k_k, k_a, r_k, lnx_g, lnx_b, hsum, mexp, mbd, mpair, tril)


def _rope_tables(t_len):
    half = HEAD_DIM // 2
    inv = ROPE_THETA ** (-jnp.arange(half, dtype=F32) / half)
    ang = jnp.arange(t_len, dtype=F32)[:, None] * inv[None, :]
    cos = jnp.cos(ang)
    sin = jnp.sin(ang)
    reps = LANES // HEAD_DIM
    cos_t = jnp.tile(jnp.concatenate([cos, cos], axis=1), (1, reps))
    sin_t = jnp.tile(jnp.concatenate([-sin, sin], axis=1), (1, reps))
    return cos_t, sin_t


def kernel(x, meta_tokens, norm_mix, norm_ffn, norm_final, ev_w_in, ev_conv_a, ev_ln_a_g, ev_ln_a_b, ev_conv_b, ev_w_out, od_w_in, od_sinks, od_mu, od_w0, od_w2, od_a0, od_a2, od_g2, od_k_k, od_k_a, od_r_k, od_lnx_g, od_lnx_b, od_w_out, ff_w_up, ff_conv, ff_conv_b, ff_w_down):
    bsz, seq, d = x.shape
    depth = norm_mix.shape[0]
    t_len = N_META + seq
    row = lambda v: v.reshape(1, -1).astype(F32)
    cos_t, sin_t = _rope_tables(t_len)
    ff_up = ff_w_up.astype(BF16)
    ff_down = ff_w_down.astype(BF16)
    ff_bias = ff_conv_b.reshape(depth, 1, D_FF)
    h = x
    for i in range(depth):
        j = i // 2
        mixer_out = None
        if i % 2 == 0:
            h = _even_mixer(h, row(norm_mix[i]), ev_w_in[j].astype(BF16),
                            jnp.repeat(ev_conv_a[j], SUBLANES, axis=0),
                            row(ev_ln_a_g[j]), row(ev_ln_a_b[j]), ev_conv_b[j],
                            ev_w_out[j].astype(BF16),
                            meta=meta_tokens.astype(F32) if i == 0 else None)
        else:
            h2d = h.reshape(bsz * t_len, d)
            qkv, pr = _odd_in_proj(h2d, row(norm_mix[i]), od_w_in[j].astype(BF16))
            y_att = _attention(qkv.reshape(bsz, t_len, ATT_COLS), cos_t, sin_t, row(od_sinks[j]))
            zeros = jnp.zeros((LORA_W, D_R), F32)
            w_lora = jnp.concatenate(
                [jnp.concatenate([od_w2[j], zeros], axis=1),
                 jnp.concatenate([zeros, od_a2[j]], axis=1)], axis=0).astype(BF16)
            y_rwkv = _rwkv(pr.reshape(bsz, t_len, RWKV_COLS), row(od_mu[j]), row(od_w0[j]),
                           row(od_a0[j]), w_lora, od_g2[j].astype(BF16), row(od_k_k[j]),
                           row(od_k_a[j]), row(od_r_k[j]), row(od_lnx_g[j]), row(od_lnx_b[j]))
            mixer_out = (y_att, y_rwkv, od_w_out[j].astype(BF16))
        last = i == depth - 1
        h = _ffn(h, row(norm_ffn[i]), ff_up, ff_conv, ff_bias, ff_down, row(norm_final),
                 layer=i, final_norm=last, mixer_out=mixer_out, drop_meta=last)
    return h
```

```python
import functools

import jax
import jax.numpy as jnp
from jax import lax
from jax.experimental import pallas as pl
from jax.experimental.pallas import tpu as pltpu

D_MODEL = 1024
N_META = 16
RMS_EPS = 1e-6
LN_EPS = 1e-5
D_A = 512
D_B = 512
CONV_A_WIDTH = 31
CONV_B_WIDTH = 3
HEAD_DIM = 64
N_Q_HEADS = 8
N_KV_HEADS = 2
GQA_GROUP = N_Q_HEADS // N_KV_HEADS
D_ATT = N_Q_HEADS * HEAD_DIM
D_KV = N_KV_HEADS * HEAD_DIM
WINDOW = 128
BLOCK = 128
ROPE_THETA = 10000.0
D_R = 512
N_R_HEADS = 8
RWKV_HEAD = 64
LORA_W = 64
LORA_A = 64
LORA_G = 128
RWKV_GN_EPS = 64e-5
ATT_COLS = D_ATT + 2 * D_KV
RWKV_COLS = 3 * D_R + LORA_W + LORA_A + LORA_G
D_FF = 2816
NEG_INF = -1e30

VMEM_LIMIT_BYTES = 56 * 1024 * 1024
LANES = 128
SUBLANES = 8
ROW_TILE = 688
FINAL_ROW_TILE = 1024
CONV_ROWS = 16
CONV_GROUP = 4
FF_CHUNK = 256
WKV_CHUNK = 16
WKV_GROUP = 16
HEADS_PER_GROUP = LANES // RWKV_HEAD
N_GROUPS = N_R_HEADS // HEADS_PER_GROUP

F32 = jnp.float32
BF16 = jnp.bfloat16


def _const_spec(shape):
    nd = len(shape)
    return pl.BlockSpec(shape, lambda *_: (0,) * nd, pipeline_mode=pl.Buffered(1))


def _layer_spec(shape, layer):
    nd = len(shape)
    return pl.BlockSpec((None,) + tuple(shape), lambda *_: (layer,) + (0,) * nd,
                        pipeline_mode=pl.Buffered(1))


def _rms_norm(x, g):
    ms = jnp.mean(x * x, axis=-1, keepdims=True)
    return x * lax.rsqrt(ms + RMS_EPS) * g


def _sigmoid(x):
    return 1.0 / (1.0 + jnp.exp(-x))


def _dot(a, b):
    return jnp.dot(a, b, preferred_element_type=F32)


def _dot_nt(a, b):
    return lax.dot_general(a, b, (((1,), (1,)), ((), ())), preferred_element_type=F32)


def _dot_tn(a, b):
    return lax.dot_general(a, b, (((0,), (0,)), ((), ())), preferred_element_type=F32)


def _scaled(i, m):
    return i * m if isinstance(i, int) else pl.multiple_of(i * m, m)


def _grouped_loop(n, group, body):
    def trip(t, carry):
        body([t * group + u for u in range(group)])
        return carry

    lax.fori_loop(0, n // group, trip, 0)
    if n % group:
        body(list(range(n - n % group, n)))


def _split_dot(x, w_bf16):
    hi = x.astype(BF16)
    lo = (x - hi.astype(F32)).astype(BF16)
    return _dot(hi, w_bf16) + _dot(lo, w_bf16)


def _split_dot_lhs(w_bf16, x):
    hi = x.astype(BF16)
    lo = (x - hi.astype(F32)).astype(BF16)
    return _dot(w_bf16, hi) + _dot(w_bf16, lo)


A_HIST = 32
B_HIST = 8


def _even_mixer_kernel(*refs, prepend_meta):
    if prepend_meta:
        h_ref, meta_ref = refs[:2]
        refs = refs[1:]
    else:
        h_ref = refs[0]
    _, g_ref, win_ref, ca_ref, lng_ref, lnb_ref, cb_ref, wout_ref, o_ref = refs[:9]
    ua_ref, ush_ref, ub_ref, y_ref = refs[9:13]
    tm = h_ref.shape[1]
    i = pl.program_id(1)

    @pl.when(i == 0)
    def _():
        ua_ref[0:A_HIST, :] = jnp.zeros((A_HIST, D_A), F32)
        ub_ref[0:B_HIST, :] = jnp.zeros((B_HIST, D_B), F32)
        ua_ref[A_HIST + tm:A_HIST + tm + SUBLANES, :] = jnp.zeros((SUBLANES, D_A), F32)

    @pl.when(i > 0)
    def _():
        ua_ref[0:A_HIST, :] = ua_ref[tm:tm + A_HIST, :]
        ub_ref[0:B_HIST, :] = ub_ref[tm:tm + B_HIST, :]

    if prepend_meta:
        xt_ref = refs[13]

        @pl.when(i == 0)
        def _():
            xt_ref[0:N_META, :] = meta_ref[...]
            xt_ref[N_META:tm, :] = h_ref[0, 0:tm - N_META, :]

        @pl.when(i > 0)
        def _():
            xt_ref[...] = h_ref[0]

        x_src = xt_ref
    else:
        x_src = h_ref.at[0]
    hn = _rms_norm(x_src[...], g_ref[...]).astype(BF16)
    a_val = _dot(hn, win_ref[:, 0:D_A])
    a_gate = _dot(hn, win_ref[:, D_A:2 * D_A])
    ua_ref[A_HIST:A_HIST + tm, :] = a_val * _sigmoid(a_gate)
    g_c = _dot(hn, win_ref[:, 2 * D_A + D_B:2 * D_A + 2 * D_B])
    x_in = _dot(hn, win_ref[:, 2 * D_A + 2 * D_B:2 * D_A + 3 * D_B])
    ub_ref[B_HIST:B_HIST + tm, :] = g_c * x_in

    conv_b = jnp.zeros((tm, D_B), F32)
    for j in range(CONV_B_WIDTH):
        off = B_HIST - (CONV_B_WIDTH - 1) + j
        conv_b = conv_b + cb_ref[j:j + 1, :] * ub_ref[off:off + tm, :]
    g_b = _dot(hn, win_ref[:, 2 * D_A:2 * D_A + D_B])
    y_ref[:, D_A:D_A + D_B] = (g_b * conv_b).astype(BF16)

    n_sh = ush_ref.shape[1]
    for s in range(SUBLANES):
        ush_ref[s] = ua_ref[s:s + n_sh, :]

    lng = lng_ref[...]
    lnb = lnb_ref[...]
    first = A_HIST - (CONV_A_WIDTH - 1)

    def conv_blocks(blocks):
        starts = [_scaled(ci, CONV_ROWS) for ci in blocks]
        accs = [jnp.zeros((CONV_ROWS, D_A), F32) for _ in blocks]
        for j in range(CONV_A_WIDTH):
            q, s = divmod(first + j, SUBLANES)
            w8 = ca_ref[j * SUBLANES:(j + 1) * SUBLANES, :]
            w = jnp.concatenate([w8] * (CONV_ROWS // SUBLANES), axis=0)
            for u, r0 in enumerate(starts):
                accs[u] = accs[u] + w * ush_ref[s, pl.ds(r0 + q * SUBLANES, CONV_ROWS), :]
        for acc, r0 in zip(accs, starts):
            mu = jnp.mean(acc, axis=-1, keepdims=True)
            cen = acc - mu
            var = jnp.mean(cen * cen, axis=-1, keepdims=True)
            ya = cen * lax.rsqrt(var + LN_EPS) * lng + lnb
            ya = ya * _sigmoid(ya)
            y_ref[pl.ds(r0, CONV_ROWS), 0:D_A] = ya.astype(BF16)

    n_blocks = tm // CONV_ROWS
    for b0 in range(0, n_blocks, CONV_GROUP):
        conv_blocks(list(range(b0, min(b0 + CONV_GROUP, n_blocks))))
    o_ref[0] = x_src[...] + _dot(y_ref[...], wout_ref[...])


def _even_mixer(h, g, w_in, conv_a, ln_g, ln_b, conv_b, w_out, meta=None):
    bsz, rows_in, d = h.shape
    tm = ROW_TILE
    n_in = w_in.shape[1]
    scratch = [
        pltpu.VMEM((A_HIST + tm + SUBLANES, D_A), F32),
        pltpu.VMEM((SUBLANES, A_HIST + tm, D_A), F32),
        pltpu.VMEM((B_HIST + tm, D_B), F32),
        pltpu.VMEM((tm, D_A + D_B), BF16),
    ]
    if meta is None:
        t_len = rows_in
        h_specs = [pl.BlockSpec((1, tm, d), lambda b, i: (b, i, 0))]
        h_args = (h,)
    else:
        t_len = rows_in + N_META
        h_specs = [pl.BlockSpec((pl.Element(1), pl.Element(tm), pl.Element(d)),
                                lambda b, i: (b, pl.multiple_of(
                                    jnp.maximum(i * tm - N_META, 0), SUBLANES), 0)),
                   _const_spec((N_META, d))]
        h_args = (h, meta)
        scratch.append(pltpu.VMEM((tm, d), F32))
    nt = t_len // tm
    return pl.pallas_call(
        functools.partial(_even_mixer_kernel, prepend_meta=meta is not None),
        out_shape=jax.ShapeDtypeStruct((bsz, t_len, d), F32),
        grid=(bsz, nt),
        in_specs=[
            *h_specs,
            _const_spec((1, d)),
            _const_spec((d, n_in)),
            _const_spec((CONV_A_WIDTH * SUBLANES, D_A)),
            _const_spec((1, D_A)),
            _const_spec((1, D_A)),
            _const_spec((CONV_B_WIDTH, D_B)),
            _const_spec((D_A + D_B, d)),
        ],
        out_specs=pl.BlockSpec((1, tm, d), lambda b, i: (b, i, 0)),
        scratch_shapes=scratch,
        compiler_params=pltpu.CompilerParams(
            dimension_semantics=("arbitrary", "arbitrary"),
            vmem_limit_bytes=VMEM_LIMIT_BYTES),
        name="even_mixer",
    )(*h_args, g, w_in, conv_a, ln_g, ln_b, conv_b, w_out)


F_HIST = 8


def _ffn_kernel(*refs, final_norm, mixer_out, halo):
    if mixer_out:
        h_ref, ya_ref, yr_ref, wo_ref = refs[:4]
        refs = refs[3:]
    else:
        h_ref = refs[0]
    _, g_ref, wup_ref, cw_ref, cb_ref, wdn_ref, gf_ref, o_ref = refs[:8]
    if halo:
        (act_ref,) = refs[8:]
    else:
        carry_ref, act_ref = refs[8:]
    tm = h_ref.shape[1]
    i = pl.program_id(1)
    n_chunks = D_FF // FF_CHUNK

    if not halo:
        @pl.when(i == 0)
        def _():
            carry_ref[...] = jnp.zeros(carry_ref.shape, F32)

    x = h_ref[0]
    if mixer_out:
        x = (x + _dot(ya_ref[0], wo_ref[0:D_ATT, :]) + _dot(yr_ref[0], wo_ref[D_ATT:D_ATT + D_R, :]))
    hn = _rms_norm(x, g_ref[...]).astype(BF16)

    def up_proj(c):
        c0 = c * FF_CHUNK
        return (_dot(hn, wup_ref[:, c0:c0 + FF_CHUNK]),
                _dot(hn, wup_ref[:, D_FF + c0:D_FF + c0 + FF_CHUNK]))

    nxt = up_proj(0)
    for c in range(n_chunks):
        g_raw, val = nxt
        if c + 1 < n_chunks:
            nxt = up_proj(c + 1)
        cols = slice(c * FF_CHUNK, (c + 1) * FF_CHUNK)
        if halo:
            prev1 = pltpu.roll(g_raw, 1, axis=0)
            prev2 = pltpu.roll(g_raw, 2, axis=0)
        else:
            ext = jnp.concatenate([carry_ref[:, cols], g_raw], axis=0)
            carry_ref[:, cols] = g_raw[tm - F_HIST:tm, :]
            prev1 = pltpu.roll(ext, 1, axis=0)[F_HIST:, :]
            prev2 = pltpu.roll(ext, 2, axis=0)[F_HIST:, :]
        gate = (cb_ref[:, cols] + cw_ref[2:3, cols] * g_raw
                + cw_ref[1:2, cols] * prev1 + cw_ref[0:1, cols] * prev2)
        act_ref[:, cols] = (gate * _sigmoid(gate) * val).astype(BF16)

    out = x + _dot(act_ref[...], wdn_ref[...])
    if final_norm:
        out = _rms_norm(out, gf_ref[...])
    o_ref[0] = out[halo:, :]


def _ffn(h, g, w_up, conv_w, conv_b, w_down, g_final, layer, final_norm, mixer_out=None,
         drop_meta=False):
    bsz, t_len, d = h.shape
    if drop_meta:
        halo = N_META
        out_rows = FINAL_ROW_TILE
        nt = (t_len - N_META) // out_rows
        tm = out_rows + halo

        def rows_spec(width):
            return pl.BlockSpec((pl.Element(1), pl.Element(tm), pl.Element(width)),
                                lambda b, i: (b, i * out_rows, 0))
        out_len = t_len - N_META
    else:
        halo = 0
        out_rows = tm = ROW_TILE
        nt = t_len // tm

        def rows_spec(width):
            return pl.BlockSpec((1, tm, width), lambda b, i: (b, i, 0))
        out_len = t_len
    mix_args, mix_specs = (), []
    if mixer_out is not None:
        y_att, y_rwkv, w_out = mixer_out
        mix_args = (y_att, y_rwkv, w_out)
        mix_specs = [rows_spec(D_ATT), rows_spec(D_R), _const_spec((D_ATT + D_R, d))]
    scratch = [] if halo else [pltpu.VMEM((F_HIST, D_FF), F32)]
    return pl.pallas_call(
        functools.partial(_ffn_kernel, final_norm=final_norm, mixer_out=mixer_out is not None,
                          halo=halo),
        out_shape=jax.ShapeDtypeStruct((bsz, out_len, d), F32),
        grid=(bsz, nt),
        in_specs=[
            rows_spec(d),
            *mix_specs,
            _const_spec((1, d)),
            _layer_spec((d, 2 * D_FF), layer),
            _layer_spec((3, D_FF), layer),
            _layer_spec((1, D_FF), layer),
            _layer_spec((D_FF, d), layer),
            _const_spec((1, d)),
        ],
        out_specs=pl.BlockSpec((1, out_rows, d), lambda b, i: (b, i, 0)),
        scratch_shapes=scratch + [pltpu.VMEM((tm, D_FF), BF16)],
        compiler_params=pltpu.CompilerParams(
            dimension_semantics=("arbitrary", "arbitrary"),
            vmem_limit_bytes=VMEM_LIMIT_BYTES),
        name="ffn_final" if final_norm else "ffn",
    )(h, *mix_args, g, w_up, conv_w, conv_b, w_down, g_final)


def _rwkv_in_kernel(h_ref, g_ref, w_ref, pr_ref):
    hn = _rms_norm(h_ref[...], g_ref[...]).astype(BF16)
    pr_ref[...] = _dot(hn, w_ref[...])


def _rwkv_in_proj(h2d, g, w_rwkv):
    rows, d = h2d.shape
    tm = ROW_TILE
    return pl.pallas_call(
        _rwkv_in_kernel,
        out_shape=jax.ShapeDtypeStruct((rows, RWKV_COLS), F32),
        grid=(rows // tm,),
        in_specs=[
            pl.BlockSpec((tm, d), lambda i: (i, 0)),
            _const_spec((1, d)),
            _const_spec((d, RWKV_COLS)),
        ],
        out_specs=pl.BlockSpec((tm, RWKV_COLS), lambda i: (i, 0)),
        compiler_params=pltpu.CompilerParams(
            dimension_semantics=("arbitrary",),
            vmem_limit_bytes=VMEM_LIMIT_BYTES),
        name="rwkv_in_proj",
    )(h2d, g, w_rwkv)


def _rope(x, cos, sin):
    lane = lax.broadcasted_iota(jnp.int32, x.shape, 1)
    first_half = (lane % HEAD_DIM) < (HEAD_DIM // 2)
    rot = jnp.where(first_half,
                    pltpu.roll(x, LANES - HEAD_DIM // 2, axis=1),
                    pltpu.roll(x, HEAD_DIM // 2, axis=1))
    return x * cos + rot * sin


K_PAD = BLOCK - N_META
N_EXTRA = 2 * N_META
N_KEYS = 2 * BLOCK + N_EXTRA
SINK_COL = 2 * BLOCK + N_META
Q_ROWS = GQA_GROUP * BLOCK
ATTN_GROUP = 4
V_WIDTH = 2 * LANES


def _attn_kernel(h_ref, g_ref, w_ref, cos_ref, sin_ref, bias_ref, mbias_ref, o_ref,
                 q_ref, k_ref, v_ref, kx_ref, vx_ref):
    t_len = h_ref.shape[1]
    scale = HEAD_DIM ** -0.5
    rt = ROW_TILE
    heads_per_group = LANES // HEAD_DIM

    k_ref[:, 0:K_PAD, :] = jnp.zeros((N_KV_HEADS, K_PAD, HEAD_DIM), BF16)
    v_ref[:, 0:K_PAD, :] = jnp.zeros((N_KV_HEADS, K_PAD, V_WIDTH), BF16)

    def project(ti):
        hn = _rms_norm(h_ref[0, ti * rt:(ti + 1) * rt, :], g_ref[...]).astype(BF16)
        return _dot(hn, w_ref[...])

    def rope_rows(ti, qkv):
        r0 = ti * rt
        cos = cos_ref[r0:r0 + rt, :]
        sin = sin_ref[r0:r0 + rt, :]
        for gq in range(D_ATT // LANES):
            xq = qkv[:, gq * LANES:(gq + 1) * LANES]
            xq = (_rope(xq, cos, sin) * scale).astype(BF16)
            for u in range(heads_per_group):
                q_ref[gq * heads_per_group + u, r0:r0 + rt, :] = xq[:, u * HEAD_DIM:(u + 1) * HEAD_DIM]
        xk = _rope(qkv[:, D_ATT:D_ATT + D_KV], cos, sin).astype(BF16)
        xv = qkv[:, D_ATT + D_KV:ATT_COLS].astype(BF16)
        ones = jnp.ones((rt, LANES), BF16)
        for u in range(N_KV_HEADS):
            k_ref[u, K_PAD + r0:K_PAD + r0 + rt, :] = xk[:, u * HEAD_DIM:(u + 1) * HEAD_DIM]
            vu = xv[:, u * HEAD_DIM:(u + 1) * HEAD_DIM]
            v_ref[u, K_PAD + r0:K_PAD + r0 + rt, :] = jnp.concatenate([vu, vu, ones], axis=1)

    n_tiles = t_len // rt
    nxt = project(0)
    for ti in range(n_tiles):
        qkv = nxt
        if ti + 1 < n_tiles:
            nxt = project(ti + 1)
        rope_rows(ti, qkv)

    kx_ref[:, 0:N_META, :] = k_ref[:, K_PAD:K_PAD + N_META, :]
    kx_ref[:, N_META:N_EXTRA, :] = jnp.zeros((N_KV_HEADS, N_EXTRA - N_META, HEAD_DIM), BF16)
    vx_ref[:, 0:N_META, :] = v_ref[:, K_PAD:K_PAD + N_META, :]
    vx_ref[:, N_META:N_EXTRA, :] = jnp.concatenate(
        [jnp.zeros((N_KV_HEADS, N_EXTRA - N_META, LANES), BF16),
         jnp.ones((N_KV_HEADS, N_EXTRA - N_META, LANES), BF16)], axis=2)

    def softmax_pv(items, n_q):
        s = []
        for g, q_start, keys, _, bias in items:
            q4 = q_ref[g * GQA_GROUP:(g + 1) * GQA_GROUP, pl.ds(q_start, n_q), :]
            s.append(_dot_nt(q4.reshape(GQA_GROUP * n_q, HEAD_DIM), keys) + bias)
        p = [jnp.exp(x - jnp.max(x, axis=-1, keepdims=True)).astype(BF16) for x in s]
        r = [_dot(p[i], items[i][3]) for i in range(len(items))]
        lane = lax.broadcasted_iota(jnp.int32, (n_q, LANES), 1)
        for (g, q_start, _, _, _), ri in zip(items, r):
            o = ri[:, 0:LANES] / ri[:, LANES:V_WIDTH]
            for u in range(GQA_GROUP // heads_per_group):
                pair = jnp.where(lane < HEAD_DIM, o[2 * u * n_q:(2 * u + 1) * n_q, :],
                                 o[(2 * u + 1) * n_q:(2 * u + 2) * n_q, :])
                col = (g * GQA_GROUP + 2 * u) * HEAD_DIM
                o_ref[0, pl.ds(q_start, n_q), col:col + LANES] = pair.astype(BF16)

    softmax_pv([(g, 0, kx_ref[g], vx_ref[g], mbias_ref[g]) for g in range(N_KV_HEADS)], N_META)

    def blocks(ns):
        items = []
        for n in ns:
            band = _scaled(n, BLOCK)
            variant = min(n, 1) if isinstance(n, int) else jnp.minimum(n, 1)
            for g in range(N_KV_HEADS):
                keys = jnp.concatenate([k_ref[g, pl.ds(band, 2 * BLOCK), :], kx_ref[g]], axis=0)
                vals = jnp.concatenate([v_ref[g, pl.ds(band, 2 * BLOCK), :], vx_ref[g]], axis=0)
                items.append((g, N_META + band, keys, vals, bias_ref[variant, g]))
        softmax_pv(items, BLOCK)

    _grouped_loop((t_len - N_META) // BLOCK, ATTN_GROUP, blocks)


def _attention(h, g, w_qkv, cos, sin, sinks):
    bsz, t_len, d = h.shape
    def stacked_bias(n_q, visible):
        sink = sinks.reshape(N_KV_HEADS, GQA_GROUP, 1, 1).astype(F32)
        base = jnp.where(visible, 0.0, NEG_INF).astype(F32)
        col = jnp.arange(visible.shape[1])[None, :]
        tiled = jnp.broadcast_to(base[None, None], (N_KV_HEADS, GQA_GROUP) + base.shape)
        tiled = jnp.where(col == visible.shape[1] - N_EXTRA + N_META, sink, tiled)
        return tiled.reshape(N_KV_HEADS, GQA_GROUP * n_q, visible.shape[1])

    qi = jnp.arange(BLOCK)[:, None]
    kj = jnp.arange(N_KEYS)[None, :]
    in_window = (kj >= qi + 1) & (kj <= qi + WINDOW) & (kj < 2 * BLOCK)
    is_meta = (kj >= 2 * BLOCK) & (kj < 2 * BLOCK + N_META)
    bias = jnp.stack([stacked_bias(BLOCK, (in_window & (kj >= BLOCK)) | is_meta),
                      stacked_bias(BLOCK, in_window | is_meta)])
    mi = jnp.arange(N_META)[:, None]
    mj = jnp.arange(N_EXTRA)[None, :]
    mbias = stacked_bias(N_META, mj <= mi)
    return pl.pallas_call(
        _attn_kernel,
        out_shape=jax.ShapeDtypeStruct((bsz, t_len, D_ATT), BF16),
        grid=(bsz,),
        in_specs=[
            pl.BlockSpec((1, t_len, d), lambda b: (b, 0, 0)),
            _const_spec((1, d)),
            _const_spec((d, ATT_COLS)),
            _const_spec((t_len, LANES)),
            _const_spec((t_len, LANES)),
            _const_spec((2, N_KV_HEADS, Q_ROWS, N_KEYS)),
            _const_spec((N_KV_HEADS, GQA_GROUP * N_META, N_EXTRA)),
        ],
        out_specs=pl.BlockSpec((1, t_len, D_ATT), lambda b: (b, 0, 0)),
        scratch_shapes=[
            pltpu.VMEM((N_Q_HEADS, t_len, HEAD_DIM), BF16),
            pltpu.VMEM((N_KV_HEADS, K_PAD + t_len, HEAD_DIM), BF16),
            pltpu.VMEM((N_KV_HEADS, K_PAD + t_len, V_WIDTH), BF16),
            pltpu.VMEM((N_KV_HEADS, N_EXTRA, HEAD_DIM), BF16),
            pltpu.VMEM((N_KV_HEADS, N_EXTRA, V_WIDTH), BF16),
        ],
        compiler_params=pltpu.CompilerParams(
            dimension_semantics=("arbitrary",),
            vmem_limit_bytes=VMEM_LIMIT_BYTES),
        name="swa_attention",
    )(h, g, w_qkv, cos, sin, bias, mbias)


def _rwkv_kernel(pr_ref, mu_ref, w0_ref, a0_ref, wlora_ref, g2_ref, kk_ref, ka_ref, rk_ref,
                 lng_ref, lnb_ref, hsum_ref, mexp_ref, mbd_ref, mpair_ref, tril_ref,
                 o_ref,
                 prev_ref, state_ref, logw_ref, r_ref, kn_ref, b_ref, k2_ref, v_ref,
                 rp_ref, yi_ref, gm_ref, hm_ref, y_ref, bonus_ref, gate_ref):
    tm = pr_ref.shape[1]
    i = pl.program_id(1)
    L = WKV_CHUNK

    @pl.when(i == 0)
    def _():
        prev_ref[...] = jnp.zeros(prev_ref.shape, F32)
        state_ref[...] = jnp.zeros(state_ref.shape, F32)

    x = pr_ref[0]
    row = lax.broadcasted_iota(jnp.int32, (tm, 1), 0)
    prev = jnp.where(row == 0, prev_ref[...], pltpu.roll(x, 1, axis=0))
    prev_ref[...] = x[tm - 1:tm, :]
    xm = x + (prev - x) * mu_ref[...]
    r = xm[:, 0:D_R]
    k = xm[:, D_R:2 * D_R]
    v = xm[:, 2 * D_R:3 * D_R]
    lora_in = xm[:, 3 * D_R:3 * D_R + LORA_W + LORA_A]
    lane = lax.broadcasted_iota(jnp.int32, lora_in.shape, 1)
    lora_in = jnp.where(lane < LORA_W, jnp.tanh(lora_in), lora_in).astype(BF16)
    lora = _dot(lora_in, wlora_ref[...])
    z = -(w0_ref[...] + lora[:, 0:D_R])
    softplus = jnp.maximum(z, 0.0) + jnp.log(1.0 + jnp.exp(-jnp.abs(z)))
    logw = -jnp.exp(-softplus - 0.5)
    alpha = _sigmoid(a0_ref[...] + lora[:, D_R:2 * D_R])
    gd = xm[:, 3 * D_R + LORA_W + LORA_A:RWKV_COLS]
    gate_ref[...] = _dot(_sigmoid(gd).astype(BF16), g2_ref[...])

    kk = k * kk_ref[...]
    norm = jnp.sqrt(_split_dot(kk * kk, hsum_ref[...]))
    kk = kk / jnp.maximum(norm, 1e-12)
    k2 = k * (1.0 + (alpha - 1.0) * ka_ref[...])
    bonus_ref[...] = _split_dot(r * k2 * rk_ref[...], hsum_ref[...]) * v
    logw_ref[...] = logw
    r_ref[...] = r
    kn_ref[...] = kk
    b_ref[...] = kk * alpha
    k2_ref[...] = k2
    v_ref[...] = v.astype(BF16)

    s_idx = lax.broadcasted_iota(jnp.int32, (L, LANES), 0)
    r_idx = lax.broadcasted_iota(jnp.int32, (L, LANES), 1) % L
    strict = r_idx < s_idx
    incl = r_idx <= s_idx
    eye_p = (r_idx == s_idx).astype(F32)
    d_row = lax.broadcasted_iota(jnp.int32, (LANES, LANES), 0)
    d_col = lax.broadcasted_iota(jnp.int32, (LANES, LANES), 1)
    diag = d_row == d_col
    n_rep = LANES // L
    tril = tril_ref[...]

    def expand(xb):
        return jnp.concatenate([xb] * n_rep, axis=0) * mexp_ref[...]

    def bdot(lhs, yp):
        ybd = jnp.concatenate([yp.astype(BF16)] * n_rep, axis=0) * mbd_ref[...]
        return _dot(lhs.astype(BF16), ybd)

    def stack(top, bottom):
        return jnp.concatenate([top.astype(BF16), bottom.astype(BF16)], axis=0)

    def advance(c, slot, j):
        def step():
            r0 = _scaled(c, L)
            for g in range(N_GROUPS):
                sl = slice(g * LANES, (g + 1) * LANES)
                s2 = state_ref[g].astype(BF16)
                out = _dot(jnp.concatenate([rp_ref[slot, j * L:(j + 1) * L, sl], gm_ref[slot, j, g]],
                                           axis=0), s2)
                y_ref[pl.ds(r0, L), sl] = out[0:L, :] + yi_ref[slot, j * L:(j + 1) * L, sl]
                state_ref[g] = out[L:, :] + hm_ref[slot, j, g]
        return step

    def process(chunks, slot, pending):
        pending = list(pending)

        def tick():
            if pending:
                pending.pop(0)()

        n = range(len(chunks))
        r0 = [_scaled(c, L) for c in chunks]
        at, rt, bt, kt, bh, kh, vv, wl = [], [], [], [], [], [], [], []
        for r in r0:
            lw = logw_ref[pl.ds(r, L), :]
            g_inc = _split_dot_lhs(tril, lw)
            e_in = jnp.exp(g_inc)
            e_neg = jnp.exp(-g_inc)
            e_last = jnp.exp(g_inc[L - 1:L, :] - g_inc)
            kn = kn_ref[pl.ds(r, L), :]
            b = b_ref[pl.ds(r, L), :]
            k2 = k2_ref[pl.ds(r, L), :]
            at.append((-kn * jnp.exp(g_inc - lw)).astype(BF16))
            rt.append(r_ref[pl.ds(r, L), :] * e_in)
            bt.append((b * e_neg).astype(BF16))
            kt.append((k2 * e_neg).astype(BF16))
            bh.append((b * e_last).astype(BF16))
            kh.append((k2 * e_last).astype(BF16))
            vv.append(v_ref[pl.ds(r, L), :])
            wl.append(e_in[L - 1:L, :])
        tick()
        sc = [_dot_nt(stack(at[i], rt[i]),
                      jnp.concatenate([expand(bt[i]), expand(kt[i])], axis=0)) for i in n]
        tick()
        a_ab = [jnp.where(strict, s[0:L, 0:LANES], 0.0) for s in sc]
        a_ak = [jnp.where(strict, s[0:L, LANES:2 * LANES], 0.0) for s in sc]
        m_rb = [jnp.where(incl, s[L:2 * L, 0:LANES], 0.0) for s in sc]
        m_rk = [jnp.where(incl, s[L:2 * L, LANES:2 * LANES], 0.0) for s in sc]
        p = [eye_p + a for a in a_ab]
        apow = [bdot(a, a) for a in a_ab]
        tick()
        for _ in range(2):
            st = [bdot(stack(p[i], apow[i]), apow[i]) for i in n]
            p = [p[i] + st[i][0:L, :] for i in n]
            apow = [s[L:2 * L, :] for s in st]
            tick()
        p = [p[i] + bdot(p[i], apow[i]) for i in n]
        tick()
        mp = [bdot(m_rb[i], p[i]) for i in n]
        tick()
        pm = [stack(p[i], mp[i]) for i in n]
        st = [bdot(pm[i], a_ak[i]) for i in n]
        pq = [stack(st[i][0:L, :], st[i][L:2 * L, :] + m_rk[i]) for i in n]
        tick()
        st = [_dot(pm[i], expand(at[i])) for i in n]
        atp = [s[0:L, :].astype(BF16) for s in st]
        for i in n:
            rp_ref[slot, i * L:(i + 1) * L, :] = (rt[i] + st[i][L:2 * L, :]).astype(BF16)
        tick()
        st = [_dot(pq[i], expand(vv[i])) for i in n]
        uv = [s[0:L, :].astype(BF16) for s in st]
        for i in n:
            yi_ref[slot, i * L:(i + 1) * L, :] = st[i][L:2 * L, :]
        tick()
        for i in n:
            for g in range(N_GROUPS):
                sl = slice(g * LANES, (g + 1) * LANES)
                rhs = jnp.concatenate(
                    [jnp.concatenate([atp[i][:, sl], uv[i][:, sl]], axis=1),
                     jnp.concatenate([jnp.zeros((L, LANES), BF16), vv[i][:, sl]], axis=1)], axis=0)
                out = _dot_tn(jnp.concatenate([bh[i][:, sl], kh[i][:, sl]], axis=0), rhs)
                gm = out[:, 0:LANES] * mpair_ref[...] + jnp.where(diag, wl[i][:, sl], 0.0)
                gm_ref[slot, i, g] = gm.astype(BF16)
                hm_ref[slot, i, g] = out[:, LANES:2 * LANES] * mpair_ref[...]
            tick()
        while pending:
            pending.pop(0)()

    n_chunks = tm // L
    n_full = n_chunks // WKV_GROUP
    rem = n_chunks % WKV_GROUP
    process(list(range(WKV_GROUP)), 0, [])

    def trip(t, carry):
        slot = t % 2
        process([t * WKV_GROUP + u for u in range(WKV_GROUP)], slot,
                [advance((t - 1) * WKV_GROUP + u, 1 - slot, u) for u in range(WKV_GROUP)])
        return carry

    lax.fori_loop(1, n_full, trip, 0)
    last = (n_full - 1) % 2
    process([n_full * WKV_GROUP + u for u in range(rem)], 1 - last,
            [advance((n_full - 1) * WKV_GROUP + u, last, u) for u in range(WKV_GROUP)])
    process([], 1 - last, [advance(n_full * WKV_GROUP + u, 1 - last, u) for u in range(rem)])

    y = y_ref[...]
    mean = _split_dot(y, hsum_ref[...]) * (1.0 / RWKV_HEAD)
    cen = y - mean
    var = _split_dot(cen * cen, hsum_ref[...]) * (1.0 / RWKV_HEAD)
    yn = cen * lax.rsqrt(var + RWKV_GN_EPS) * lng_ref[...] + lnb_ref[...]
    o_ref[0] = ((yn + bonus_ref[...]) * gate_ref[...]).astype(BF16)


def _rwkv(pr, mu, w0, a0, w_lora, g2, k_k, k_a, r_k, lnx_g, lnx_b):
    bsz, t_len, _ = pr.shape
    tm = ROW_TILE
    nt = t_len // tm
    assert (tm // WKV_CHUNK) // WKV_GROUP >= 1 and tm % WKV_CHUNK == 0
    group_rows = WKV_GROUP * WKV_CHUNK

    lane_head = jnp.arange(D_R) // RWKV_HEAD
    hsum = (lane_head[:, None] == lane_head[None, :]).astype(BF16)
    row_head = jnp.arange(LANES) // WKV_CHUNK
    mexp = (row_head[:, None] == lane_head[None, :]).astype(BF16)
    mbd = (row_head[:, None] == row_head[None, :]).astype(BF16)
    pair_head = jnp.arange(LANES) // RWKV_HEAD
    mpair = (pair_head[:, None] == pair_head[None, :]).astype(F32)
    tril = jnp.tril(jnp.ones((WKV_CHUNK, WKV_CHUNK), F32)).astype(BF16)

    tile_f32 = pltpu.VMEM((tm, D_R), F32)
    tile_bf16 = pltpu.VMEM((tm, D_R), BF16)
    return pl.pallas_call(
        _rwkv_kernel,
        out_shape=jax.ShapeDtypeStruct((bsz, t_len, D_R), BF16),
        grid=(bsz, nt),
        in_specs=[
            pl.BlockSpec((1, tm, RWKV_COLS), lambda b, i: (b, i, 0)),
            _const_spec((1, RWKV_COLS)),
            _const_spec((1, D_R)),
            _const_spec((1, D_R)),
            _const_spec((LORA_W + LORA_A, 2 * D_R)),
            _const_spec((LORA_G, D_R)),
            _const_spec((1, D_R)),
            _const_spec((1, D_R)),
            _const_spec((1, D_R)),
            _const_spec((1, D_R)),
            _const_spec((1, D_R)),
            _const_spec((D_R, D_R)),
            _const_spec((LANES, D_R)),
            _const_spec((LANES, LANES)),
            _const_spec((LANES, LANES)),
            _const_spec((WKV_CHUNK, WKV_CHUNK)),
        ],
        out_specs=pl.BlockSpec((1, tm, D_R), lambda b, i: (b, i, 0)),
        scratch_shapes=[
            pltpu.VMEM((1, RWKV_COLS), F32),
            pltpu.VMEM((N_GROUPS, LANES, LANES), F32),
            tile_f32, tile_f32, tile_f32, tile_f32, tile_f32, tile_bf16,
            pltpu.VMEM((2, group_rows, D_R), BF16),
            pltpu.VMEM((2, group_rows, D_R), F32),
            pltpu.VMEM((2, WKV_GROUP, N_GROUPS, LANES, LANES), BF16),
            pltpu.VMEM((2, WKV_GROUP, N_GROUPS, LANES, LANES), F32),
            tile_f32, tile_f32, tile_f32,
        ],
        compiler_params=pltpu.CompilerParams(
            dimension_semantics=("arbitrary", "arbitrary"),
            vmem_limit_bytes=VMEM_LIMIT_BYTES),
        name="rwkv7",
    )(pr, mu, w0, a0, w_lora, g2, k_k, k_a, r_k, lnx_g, lnx_b, hsum, mexp, mbd, mpair, tril)


def _rope_tables(t_len):
    half = HEAD_DIM // 2
    inv = ROPE_THETA ** (-jnp.arange(half, dtype=F32) / half)
    ang = jnp.arange(t_len, dtype=F32)[:, None] * inv[None, :]
    cos = jnp.cos(ang)
    sin = jnp.sin(ang)
    reps = LANES // HEAD_DIM
    cos_t = jnp.tile(jnp.concatenate([cos, cos], axis=1), (1, reps))
    sin_t = jnp.tile(jnp.concatenate([-sin, sin], axis=1), (1, reps))
    return cos_t, sin_t


def kernel(x, meta_tokens, norm_mix, norm_ffn, norm_final, ev_w_in, ev_conv_a, ev_ln_a_g, ev_ln_a_b, ev_conv_b, ev_w_out, od_w_in, od_sinks, od_mu, od_w0, od_w2, od_a0, od_a2, od_g2, od_k_k, od_k_a, od_r_k, od_lnx_g, od_lnx_b, od_w_out, ff_w_up, ff_conv, ff_conv_b, ff_w_down):
    bsz, seq, d = x.shape
    depth = norm_mix.shape[0]
    t_len = N_META + seq
    row = lambda v: v.reshape(1, -1).astype(F32)
    cos_t, sin_t = _rope_tables(t_len)
    ff_up = ff_w_up.astype(BF16)
    ff_down = ff_w_down.astype(BF16)
    ff_bias = ff_conv_b.reshape(depth, 1, D_FF)
    h = x
    for i in range(depth):
        j = i // 2
        mixer_out = None
        if i % 2 == 0:
            h = _even_mixer(h, row(norm_mix[i]), ev_w_in[j].astype(BF16),
                            jnp.repeat(ev_conv_a[j], SUBLANES, axis=0),
                            row(ev_ln_a_g[j]), row(ev_ln_a_b[j]), ev_conv_b[j],
                            ev_w_out[j].astype(BF16),
                            meta=meta_tokens.astype(F32) if i == 0 else None)
        else:
            h2d = h.reshape(bsz * t_len, d)
            w_in = od_w_in[j].astype(BF16)
            y_att = _attention(h, row(norm_mix[i]), w_in[:, :ATT_COLS], cos_t, sin_t,
                               row(od_sinks[j]))
            pr = _rwkv_in_proj(h2d, row(norm_mix[i]), w_in[:, ATT_COLS:])
            zeros = jnp.zeros((LORA_W, D_R), F32)
            w_lora = jnp.concatenate(
                [jnp.concatenate([od_w2[j], zeros], axis=1),
                 jnp.concatenate([zeros, od_a2[j]], axis=1)], axis=0).astype(BF16)
            y_rwkv = _rwkv(pr.reshape(bsz, t_len, RWKV_COLS), row(od_mu[j]), row(od_w0[j]),
                           row(od_a0[j]), w_lora, od_g2[j].astype(BF16), row(od_k_k[j]),
                           row(od_k_a[j]), row(od_r_k[j]), row(od_lnx_g[j]), row(od_lnx_b[j]))
            mixer_out = (y_att, y_rwkv, od_w_out[j].astype(BF16))
        last = i == depth - 1
        h = _ffn(h, row(norm_ffn[i]), ff_up, ff_conv, ff_bias, ff_down, row(norm_final),
                 layer=i, final_norm=last, mixer_out=mixer_out, drop_meta=last)
    return h
```

```python
import functools

import jax
import jax.numpy as jnp
from jax import lax
from jax.experimental import pallas as pl
from jax.experimental.pallas import tpu as pltpu

D_MODEL = 1024
N_META = 16
RMS_EPS = 1e-6
LN_EPS = 1e-5
D_A = 512
D_B = 512
CONV_A_WIDTH = 31
CONV_B_WIDTH = 3
HEAD_DIM = 64
N_Q_HEADS = 8
N_KV_HEADS = 2
GQA_GROUP = N_Q_HEADS // N_KV_HEADS
D_ATT = N_Q_HEADS * HEAD_DIM
D_KV = N_KV_HEADS * HEAD_DIM
WINDOW = 128
BLOCK = 128
ROPE_THETA = 10000.0
D_R = 512
N_R_HEADS = 8
RWKV_HEAD = 64
LORA_W = 64
LORA_A = 64
LORA_G = 128
RWKV_GN_EPS = 64e-5
ATT_COLS = D_ATT + 2 * D_KV
RWKV_COLS = 3 * D_R + LORA_W + LORA_A + LORA_G
D_FF = 2816
NEG_INF = -1e30

VMEM_LIMIT_BYTES = 56 * 1024 * 1024
LANES = 128
SUBLANES = 8
ROW_TILE = 688
FINAL_ROW_TILE = 1024
CONV_ROWS = 16
CONV_GROUP = 4
FF_CHUNK = 256
WKV_CHUNK = 16
WKV_GROUP = 16
HEADS_PER_GROUP = LANES // RWKV_HEAD
N_GROUPS = N_R_HEADS // HEADS_PER_GROUP

F32 = jnp.float32
BF16 = jnp.bfloat16


def _const_spec(shape):
    nd = len(shape)
    return pl.BlockSpec(shape, lambda *_: (0,) * nd, pipeline_mode=pl.Buffered(1))


def _layer_spec(shape, layer):
    nd = len(shape)
    return pl.BlockSpec((None,) + tuple(shape), lambda *_: (layer,) + (0,) * nd,
                        pipeline_mode=pl.Buffered(1))


def _rms_norm(x, g):
    ms = jnp.mean(x * x, axis=-1, keepdims=True)
    return x * lax.rsqrt(ms + RMS_EPS) * g


def _sigmoid(x):
    return 1.0 / (1.0 + jnp.exp(-x))


def _dot(a, b):
    return jnp.dot(a, b, preferred_element_type=F32)


def _dot_nt(a, b):
    return lax.dot_general(a, b, (((1,), (1,)), ((), ())), preferred_element_type=F32)


def _dot_tn(a, b):
    return lax.dot_general(a, b, (((0,), (0,)), ((), ())), preferred_element_type=F32)


def _scaled(i, m):
    return i * m if isinstance(i, int) else pl.multiple_of(i * m, m)


def _grouped_loop(n, group, body):
    def trip(t, carry):
        body([t * group + u for u in range(group)])
        return carry

    lax.fori_loop(0, n // group, trip, 0)
    if n % group:
        body(list(range(n - n % group, n)))


def _split_dot(x, w_bf16):
    hi = x.astype(BF16)
    lo = (x - hi.astype(F32)).astype(BF16)
    return _dot(hi, w_bf16) + _dot(lo, w_bf16)


def _split_dot_lhs(w_bf16, x):
    hi = x.astype(BF16)
    lo = (x - hi.astype(F32)).astype(BF16)
    return _dot(w_bf16, hi) + _dot(w_bf16, lo)


A_HIST = 32
B_HIST = 8


def _even_mixer_kernel(*refs, prepend_meta):
    if prepend_meta:
        h_ref, meta_ref = refs[:2]
        refs = refs[1:]
    else:
        h_ref = refs[0]
    _, g_ref, win_ref, ca_ref, lng_ref, lnb_ref, cb_ref, wout_ref, o_ref = refs[:9]
    ua_ref, ush_ref, ub_ref, y_ref = refs[9:13]
    tm = h_ref.shape[1]
    i = pl.program_id(1)

    @pl.when(i == 0)
    def _():
        ua_ref[0:A_HIST, :] = jnp.zeros((A_HIST, D_A), F32)
        ub_ref[0:B_HIST, :] = jnp.zeros((B_HIST, D_B), F32)
        ua_ref[A_HIST + tm:A_HIST + tm + SUBLANES, :] = jnp.zeros((SUBLANES, D_A), F32)

    @pl.when(i > 0)
    def _():
        ua_ref[0:A_HIST, :] = ua_ref[tm:tm + A_HIST, :]
        ub_ref[0:B_HIST, :] = ub_ref[tm:tm + B_HIST, :]

    if prepend_meta:
        xt_ref = refs[13]

        @pl.when(i == 0)
        def _():
            xt_ref[0:N_META, :] = meta_ref[...]
            xt_ref[N_META:tm, :] = h_ref[0, 0:tm - N_META, :]

        @pl.when(i > 0)
        def _():
            xt_ref[...] = h_ref[0]

        x_src = xt_ref
    else:
        x_src = h_ref.at[0]
    hn = _rms_norm(x_src[...], g_ref[...]).astype(BF16)
    a_val = _dot(hn, win_ref[:, 0:D_A])
    a_gate = _dot(hn, win_ref[:, D_A:2 * D_A])
    ua_ref[A_HIST:A_HIST + tm, :] = a_val * _sigmoid(a_gate)
    g_c = _dot(hn, win_ref[:, 2 * D_A + D_B:2 * D_A + 2 * D_B])
    x_in = _dot(hn, win_ref[:, 2 * D_A + 2 * D_B:2 * D_A + 3 * D_B])
    ub_ref[B_HIST:B_HIST + tm, :] = g_c * x_in

    conv_b = jnp.zeros((tm, D_B), F32)
    for j in range(CONV_B_WIDTH):
        off = B_HIST - (CONV_B_WIDTH - 1) + j
        conv_b = conv_b + cb_ref[j:j + 1, :] * ub_ref[off:off + tm, :]
    g_b = _dot(hn, win_ref[:, 2 * D_A:2 * D_A + D_B])
    y_ref[:, D_A:D_A + D_B] = (g_b * conv_b).astype(BF16)

    n_sh = ush_ref.shape[1]
    for s in range(SUBLANES):
        ush_ref[s] = ua_ref[s:s + n_sh, :]

    lng = lng_ref[...]
    lnb = lnb_ref[...]
    first = A_HIST - (CONV_A_WIDTH - 1)

    def conv_blocks(blocks):
        starts = [_scaled(ci, CONV_ROWS) for ci in blocks]
        accs = [jnp.zeros((CONV_ROWS, D_A), F32) for _ in blocks]
        for j in range(CONV_A_WIDTH):
            q, s = divmod(first + j, SUBLANES)
            w8 = ca_ref[j * SUBLANES:(j + 1) * SUBLANES, :]
            w = jnp.concatenate([w8] * (CONV_ROWS // SUBLANES), axis=0)
            for u, r0 in enumerate(starts):
                accs[u] = accs[u] + w * ush_ref[s, pl.ds(r0 + q * SUBLANES, CONV_ROWS), :]
        for acc, r0 in zip(accs, starts):
            mu = jnp.mean(acc, axis=-1, keepdims=True)
            cen = acc - mu
            var = jnp.mean(cen * cen, axis=-1, keepdims=True)
            ya = cen * lax.rsqrt(var + LN_EPS) * lng + lnb
            ya = ya * _sigmoid(ya)
            y_ref[pl.ds(r0, CONV_ROWS), 0:D_A] = ya.astype(BF16)

    n_blocks = tm // CONV_ROWS
    for b0 in range(0, n_blocks, CONV_GROUP):
        conv_blocks(list(range(b0, min(b0 + CONV_GROUP, n_blocks))))
    o_ref[0] = x_src[...] + _dot(y_ref[...], wout_ref[...])


def _even_mixer(h, g, w_in, conv_a, ln_g, ln_b, conv_b, w_out, meta=None):
    bsz, rows_in, d = h.shape
    tm = ROW_TILE
    n_in = w_in.shape[1]
    scratch = [
        pltpu.VMEM((A_HIST + tm + SUBLANES, D_A), F32),
        pltpu.VMEM((SUBLANES, A_HIST + tm, D_A), F32),
        pltpu.VMEM((B_HIST + tm, D_B), F32),
        pltpu.VMEM((tm, D_A + D_B), BF16),
    ]
    if meta is None:
        t_len = rows_in
        h_specs = [pl.BlockSpec((1, tm, d), lambda b, i: (b, i, 0))]
        h_args = (h,)
    else:
        t_len = rows_in + N_META
        h_specs = [pl.BlockSpec((pl.Element(1), pl.Element(tm), pl.Element(d)),
                                lambda b, i: (b, pl.multiple_of(
                                    jnp.maximum(i * tm - N_META, 0), SUBLANES), 0)),
                   _const_spec((N_META, d))]
        h_args = (h, meta)
        scratch.append(pltpu.VMEM((tm, d), F32))
    nt = t_len // tm
    return pl.pallas_call(
        functools.partial(_even_mixer_kernel, prepend_meta=meta is not None),
        out_shape=jax.ShapeDtypeStruct((bsz, t_len, d), F32),
        grid=(bsz, nt),
        in_specs=[
            *h_specs,
            _const_spec((1, d)),
            _const_spec((d, n_in)),
            _const_spec((CONV_A_WIDTH * SUBLANES, D_A)),
            _const_spec((1, D_A)),
            _const_spec((1, D_A)),
            _const_spec((CONV_B_WIDTH, D_B)),
            _const_spec((D_A + D_B, d)),
        ],
        out_specs=pl.BlockSpec((1, tm, d), lambda b, i: (b, i, 0)),
        scratch_shapes=scratch,
        compiler_params=pltpu.CompilerParams(
            dimension_semantics=("arbitrary", "arbitrary"),
            vmem_limit_bytes=VMEM_LIMIT_BYTES),
        name="even_mixer",
    )(*h_args, g, w_in, conv_a, ln_g, ln_b, conv_b, w_out)


F_HIST = 8


def _ffn_kernel(*refs, final_norm, mixer_out, halo):
    if mixer_out:
        h_ref, ya_ref, yr_ref, wo_ref = refs[:4]
        refs = refs[3:]
    else:
        h_ref = refs[0]
    _, g_ref, wup_ref, cw_ref, cb_ref, wdn_ref, gf_ref, o_ref = refs[:8]
    if halo:
        (act_ref,) = refs[8:]
    else:
        carry_ref, act_ref = refs[8:]
    tm = h_ref.shape[1]
    i = pl.program_id(1)
    n_chunks = D_FF // FF_CHUNK

    if not halo:
        @pl.when(i == 0)
        def _():
            carry_ref[...] = jnp.zeros(carry_ref.shape, F32)

    x = h_ref[0]
    if mixer_out:
        x = (x + _dot(ya_ref[0], wo_ref[0:D_ATT, :]) + _dot(yr_ref[0], wo_ref[D_ATT:D_ATT + D_R, :]))
    hn = _rms_norm(x, g_ref[...]).astype(BF16)

    def up_proj(c):
        c0 = c * FF_CHUNK
        return (_dot(hn, wup_ref[:, c0:c0 + FF_CHUNK]),
                _dot(hn, wup_ref[:, D_FF + c0:D_FF + c0 + FF_CHUNK]))

    nxt = up_proj(0)
    for c in range(n_chunks):
        g_raw, val = nxt
        if c + 1 < n_chunks:
            nxt = up_proj(c + 1)
        cols = slice(c * FF_CHUNK, (c + 1) * FF_CHUNK)
        if halo:
            prev1 = pltpu.roll(g_raw, 1, axis=0)
            prev2 = pltpu.roll(g_raw, 2, axis=0)
        else:
            ext = jnp.concatenate([carry_ref[:, cols], g_raw], axis=0)
            carry_ref[:, cols] = g_raw[tm - F_HIST:tm, :]
            prev1 = pltpu.roll(ext, 1, axis=0)[F_HIST:, :]
            prev2 = pltpu.roll(ext, 2, axis=0)[F_HIST:, :]
        gate = (cb_ref[:, cols] + cw_ref[2:3, cols] * g_raw
                + cw_ref[1:2, cols] * prev1 + cw_ref[0:1, cols] * prev2)
        act_ref[:, cols] = (gate * _sigmoid(gate) * val).astype(BF16)

    out = x + _dot(act_ref[...], wdn_ref[...])
    if final_norm:
        out = _rms_norm(out, gf_ref[...])
    o_ref[0] = out[halo:, :]


def _ffn(h, g, w_up, conv_w, conv_b, w_down, g_final, layer, final_norm, mixer_out=None,
         drop_meta=False):
    bsz, t_len, d = h.shape
    if drop_meta:
        halo = N_META
        out_rows = FINAL_ROW_TILE
        nt = (t_len - N_META) // out_rows
        tm = out_rows + halo

        def rows_spec(width):
            return pl.BlockSpec((pl.Element(1), pl.Element(tm), pl.Element(width)),
                                lambda b, i: (b, i * out_rows, 0))
        out_len = t_len - N_META
    else:
        halo = 0
        out_rows = tm = ROW_TILE
        nt = t_len // tm

        def rows_spec(width):
            return pl.BlockSpec((1, tm, width), lambda b, i: (b, i, 0))
        out_len = t_len
    mix_args, mix_specs = (), []
    if mixer_out is not None:
        y_att, y_rwkv, w_out = mixer_out
        mix_args = (y_att, y_rwkv, w_out)
        mix_specs = [rows_spec(D_ATT), rows_spec(D_R), _const_spec((D_ATT + D_R, d))]
    scratch = [] if halo else [pltpu.VMEM((F_HIST, D_FF), F32)]
    return pl.pallas_call(
        functools.partial(_ffn_kernel, final_norm=final_norm, mixer_out=mixer_out is not None,
                          halo=halo),
        out_shape=jax.ShapeDtypeStruct((bsz, out_len, d), F32),
        grid=(bsz, nt),
        in_specs=[
            rows_spec(d),
            *mix_specs,
            _const_spec((1, d)),
            _layer_spec((d, 2 * D_FF), layer),
            _layer_spec((3, D_FF), layer),
            _layer_spec((1, D_FF), layer),
            _layer_spec((D_FF, d), layer),
            _const_spec((1, d)),
        ],
        out_specs=pl.BlockSpec((1, out_rows, d), lambda b, i: (b, i, 0)),
        scratch_shapes=scratch + [pltpu.VMEM((tm, D_FF), BF16)],
        compiler_params=pltpu.CompilerParams(
            dimension_semantics=("arbitrary", "arbitrary"),
            vmem_limit_bytes=VMEM_LIMIT_BYTES),
        name="ffn_final" if final_norm else "ffn",
    )(h, *mix_args, g, w_up, conv_w, conv_b, w_down, g_final)


def _rwkv_in_kernel(h_ref, g_ref, w_ref, pr_ref):
    hn = _rms_norm(h_ref[...], g_ref[...]).astype(BF16)
    pr_ref[...] = _dot(hn, w_ref[...])


def _rwkv_in_proj(h2d, g, w_rwkv):
    rows, d = h2d.shape
    tm = 2 * ROW_TILE
    return pl.pallas_call(
        _rwkv_in_kernel,
        out_shape=jax.ShapeDtypeStruct((rows, RWKV_COLS), F32),
        grid=(rows // tm,),
        in_specs=[
            pl.BlockSpec((tm, d), lambda i: (i, 0)),
            _const_spec((1, d)),
            _const_spec((d, RWKV_COLS)),
        ],
        out_specs=pl.BlockSpec((tm, RWKV_COLS), lambda i: (i, 0)),
        compiler_params=pltpu.CompilerParams(
            dimension_semantics=("arbitrary",),
            vmem_limit_bytes=VMEM_LIMIT_BYTES),
        name="rwkv_in_proj",
    )(h2d, g, w_rwkv)


def _rope(x, cos, sin):
    lane = lax.broadcasted_iota(jnp.int32, x.shape, 1)
    first_half = (lane % HEAD_DIM) < (HEAD_DIM // 2)
    rot = jnp.where(first_half,
                    pltpu.roll(x, LANES - HEAD_DIM // 2, axis=1),
                    pltpu.roll(x, HEAD_DIM // 2, axis=1))
    return x * cos + rot * sin


K_PAD = BLOCK - N_META
N_EXTRA = 2 * N_META
N_KEYS = 2 * BLOCK + N_EXTRA
Q_ROWS = GQA_GROUP * BLOCK
ATTN_GROUP = 4
V_WIDTH = 2 * LANES


def _attn_kernel(h_ref, g_ref, w_ref, cos_ref, sin_ref, bias_ref, mbias_ref, o_ref,
                 q_ref, k_ref, v_ref, kx_ref, vx_ref):
    t_len = h_ref.shape[1]
    scale = HEAD_DIM ** -0.5
    rt = ROW_TILE
    heads_per_group = LANES // HEAD_DIM

    k_ref[:, 0:K_PAD, :] = jnp.zeros((N_KV_HEADS, K_PAD, HEAD_DIM), BF16)
    v_ref[:, 0:K_PAD, :] = jnp.zeros((N_KV_HEADS, K_PAD, V_WIDTH), BF16)

    def project(ti):
        hn = _rms_norm(h_ref[0, ti * rt:(ti + 1) * rt, :], g_ref[...]).astype(BF16)
        return _dot(hn, w_ref[...])

    def rope_rows(ti, qkv):
        r0 = ti * rt
        cos = cos_ref[r0:r0 + rt, :]
        sin = sin_ref[r0:r0 + rt, :]
        for gq in range(D_ATT // LANES):
            xq = qkv[:, gq * LANES:(gq + 1) * LANES]
            xq = (_rope(xq, cos, sin) * scale).astype(BF16)
            for u in range(heads_per_group):
                q_ref[gq * heads_per_group + u, r0:r0 + rt, :] = xq[:, u * HEAD_DIM:(u + 1) * HEAD_DIM]
        xk = _rope(qkv[:, D_ATT:D_ATT + D_KV], cos, sin).astype(BF16)
        xv = qkv[:, D_ATT + D_KV:ATT_COLS].astype(BF16)
        ones = jnp.ones((rt, LANES), BF16)
        for u in range(N_KV_HEADS):
            k_ref[u, K_PAD + r0:K_PAD + r0 + rt, :] = xk[:, u * HEAD_DIM:(u + 1) * HEAD_DIM]
            vu = xv[:, u * HEAD_DIM:(u + 1) * HEAD_DIM]
            v_ref[u, K_PAD + r0:K_PAD + r0 + rt, :] = jnp.concatenate([vu, vu, ones], axis=1)

    n_tiles = t_len // rt
    nxt = project(0)
    for ti in range(n_tiles):
        qkv = nxt
        if ti + 1 < n_tiles:
            nxt = project(ti + 1)
        rope_rows(ti, qkv)

    kx_ref[:, 0:N_META, :] = k_ref[:, K_PAD:K_PAD + N_META, :]
    kx_ref[:, N_META:N_EXTRA, :] = jnp.zeros((N_KV_HEADS, N_EXTRA - N_META, HEAD_DIM), BF16)
    vx_ref[:, 0:N_META, :] = v_ref[:, K_PAD:K_PAD + N_META, :]
    vx_ref[:, N_META:N_EXTRA, :] = jnp.concatenate(
        [jnp.zeros((N_KV_HEADS, N_EXTRA - N_META, LANES), BF16),
         jnp.ones((N_KV_HEADS, N_EXTRA - N_META, LANES), BF16)], axis=2)

    def softmax_pv(items, n_q):
        s = []
        for g, q_start, keys, _, bias in items:
            q4 = q_ref[g * GQA_GROUP:(g + 1) * GQA_GROUP, pl.ds(q_start, n_q), :]
            s.append(_dot_nt(q4.reshape(GQA_GROUP * n_q, HEAD_DIM), keys) + bias)
        p = [jnp.exp(x - jnp.max(x, axis=-1, keepdims=True)).astype(BF16) for x in s]
        r = [_dot(p[i], items[i][3]) for i in range(len(items))]
        lane = lax.broadcasted_iota(jnp.int32, (n_q, LANES), 1)
        for (g, q_start, _, _, _), ri in zip(items, r):
            o = ri[:, 0:LANES] / ri[:, LANES:V_WIDTH]
            for u in range(GQA_GROUP // heads_per_group):
                pair = jnp.where(lane < HEAD_DIM, o[2 * u * n_q:(2 * u + 1) * n_q, :],
                                 o[(2 * u + 1) * n_q:(2 * u + 2) * n_q, :])
                col = (g * GQA_GROUP + 2 * u) * HEAD_DIM
                o_ref[0, pl.ds(q_start, n_q), col:col + LANES] = pair.astype(BF16)

    softmax_pv([(g, 0, kx_ref[g], vx_ref[g], mbias_ref[g]) for g in range(N_KV_HEADS)], N_META)

    def blocks(ns):
        items = []
        for n in ns:
            band = _scaled(n, BLOCK)
            variant = min(n, 1) if isinstance(n, int) else jnp.minimum(n, 1)
            for g in range(N_KV_HEADS):
                keys = jnp.concatenate([k_ref[g, pl.ds(band, 2 * BLOCK), :], kx_ref[g]], axis=0)
                vals = jnp.concatenate([v_ref[g, pl.ds(band, 2 * BLOCK), :], vx_ref[g]], axis=0)
                items.append((g, N_META + band, keys, vals, bias_ref[variant, g]))
        softmax_pv(items, BLOCK)

    _grouped_loop((t_len - N_META) // BLOCK, ATTN_GROUP, blocks)


def _attention(h, g, w_qkv, cos, sin, sinks):
    bsz, t_len, d = h.shape
    def stacked_bias(n_q, visible):
        sink = sinks.reshape(N_KV_HEADS, GQA_GROUP, 1, 1).astype(F32)
        base = jnp.where(visible, 0.0, NEG_INF).astype(F32)
        col = jnp.arange(visible.shape[1])[None, :]
        tiled = jnp.broadcast_to(base[None, None], (N_KV_HEADS, GQA_GROUP) + base.shape)
        tiled = jnp.where(col == visible.shape[1] - N_EXTRA + N_META, sink, tiled)
        return tiled.reshape(N_KV_HEADS, GQA_GROUP * n_q, visible.shape[1])

    qi = jnp.arange(BLOCK)[:, None]
    kj = jnp.arange(N_KEYS)[None, :]
    in_window = (kj >= qi + 1) & (kj <= qi + WINDOW) & (kj < 2 * BLOCK)
    is_meta = (kj >= 2 * BLOCK) & (kj < 2 * BLOCK + N_META)
    bias = jnp.stack([stacked_bias(BLOCK, (in_window & (kj >= BLOCK)) | is_meta),
                      stacked_bias(BLOCK, in_window | is_meta)])
    mi = jnp.arange(N_META)[:, None]
    mj = jnp.arange(N_EXTRA)[None, :]
    mbias = stacked_bias(N_META, mj <= mi)
    return pl.pallas_call(
        _attn_kernel,
        out_shape=jax.ShapeDtypeStruct((bsz, t_len, D_ATT), BF16),
        grid=(bsz,),
        in_specs=[
            pl.BlockSpec((1, t_len, d), lambda b: (b, 0, 0)),
            _const_spec((1, d)),
            _const_spec((d, ATT_COLS)),
            _const_spec((t_len, LANES)),
            _const_spec((t_len, LANES)),
            _const_spec((2, N_KV_HEADS, Q_ROWS, N_KEYS)),
            _const_spec((N_KV_HEADS, GQA_GROUP * N_META, N_EXTRA)),
        ],
        out_specs=pl.BlockSpec((1, t_len, D_ATT), lambda b: (b, 0, 0)),
        scratch_shapes=[
            pltpu.VMEM((N_Q_HEADS, t_len, HEAD_DIM), BF16),
            pltpu.VMEM((N_KV_HEADS, K_PAD + t_len, HEAD_DIM), BF16),
            pltpu.VMEM((N_KV_HEADS, K_PAD + t_len, V_WIDTH), BF16),
            pltpu.VMEM((N_KV_HEADS, N_EXTRA, HEAD_DIM), BF16),
            pltpu.VMEM((N_KV_HEADS, N_EXTRA, V_WIDTH), BF16),
        ],
        compiler_params=pltpu.CompilerParams(
            dimension_semantics=("arbitrary",),
            vmem_limit_bytes=VMEM_LIMIT_BYTES),
        name="swa_attention",
    )(h, g, w_qkv, cos, sin, bias, mbias)


def _rwkv_kernel(pr_ref, mu_ref, w0_ref, a0_ref, wlora_ref, g2_ref, kk_ref, ka_ref, rk_ref,
                 lng_ref, lnb_ref, hsum_ref, mexp_ref, mbd_ref, mpair_ref, tril_ref,
                 o_ref,
                 prev_ref, state_ref, logw_ref, r_ref, kn_ref, b_ref, k2_ref, v_ref,
                 rp_ref, yi_ref, gm_ref, hm_ref, y_ref, bonus_ref, gate_ref):
    tm = pr_ref.shape[1]
    i = pl.program_id(1)
    L = WKV_CHUNK

    @pl.when(i == 0)
    def _():
        prev_ref[...] = jnp.zeros(prev_ref.shape, F32)
        state_ref[...] = jnp.zeros(state_ref.shape, F32)

    x = pr_ref[0]
    row = lax.broadcasted_iota(jnp.int32, (tm, 1), 0)
    prev = jnp.where(row == 0, prev_ref[...], pltpu.roll(x, 1, axis=0))
    prev_ref[...] = x[tm - 1:tm, :]
    xm = x + (prev - x) * mu_ref[...]
    r = xm[:, 0:D_R]
    k = xm[:, D_R:2 * D_R]
    v = xm[:, 2 * D_R:3 * D_R]
    lora_in = xm[:, 3 * D_R:3 * D_R + LORA_W + LORA_A]
    lane = lax.broadcasted_iota(jnp.int32, lora_in.shape, 1)
    lora_in = jnp.where(lane < LORA_W, jnp.tanh(lora_in), lora_in).astype(BF16)
    lora = _dot(lora_in, wlora_ref[...])
    z = -(w0_ref[...] + lora[:, 0:D_R])
    softplus = jnp.maximum(z, 0.0) + jnp.log(1.0 + jnp.exp(-jnp.abs(z)))
    logw = -jnp.exp(-softplus - 0.5)
    alpha = _sigmoid(a0_ref[...] + lora[:, D_R:2 * D_R])
    gd = xm[:, 3 * D_R + LORA_W + LORA_A:RWKV_COLS]
    gate_ref[...] = _dot(_sigmoid(gd).astype(BF16), g2_ref[...])

    kk = k * kk_ref[...]
    norm = jnp.sqrt(_split_dot(kk * kk, hsum_ref[...]))
    kk = kk / jnp.maximum(norm, 1e-12)
    k2 = k * (1.0 + (alpha - 1.0) * ka_ref[...])
    bonus_ref[...] = _split_dot(r * k2 * rk_ref[...], hsum_ref[...]) * v
    logw_ref[...] = logw
    r_ref[...] = r
    kn_ref[...] = kk
    b_ref[...] = kk * alpha
    k2_ref[...] = k2
    v_ref[...] = v.astype(BF16)

    s_idx = lax.broadcasted_iota(jnp.int32, (L, LANES), 0)
    r_idx = lax.broadcasted_iota(jnp.int32, (L, LANES), 1) % L
    strict = r_idx < s_idx
    incl = r_idx <= s_idx
    eye_p = (r_idx == s_idx).astype(F32)
    d_row = lax.broadcasted_iota(jnp.int32, (LANES, LANES), 0)
    d_col = lax.broadcasted_iota(jnp.int32, (LANES, LANES), 1)
    diag = d_row == d_col
    n_rep = LANES // L
    tril = tril_ref[...]

    def expand(xb):
        return jnp.concatenate([xb] * n_rep, axis=0) * mexp_ref[...]

    def bdot(lhs, yp):
        ybd = jnp.concatenate([yp.astype(BF16)] * n_rep, axis=0) * mbd_ref[...]
        return _dot(lhs.astype(BF16), ybd)

    def stack(top, bottom):
        return jnp.concatenate([top.astype(BF16), bottom.astype(BF16)], axis=0)

    def advance(c, slot, j):
        def step():
            r0 = _scaled(c, L)
            for g in range(N_GROUPS):
                sl = slice(g * LANES, (g + 1) * LANES)
                s2 = state_ref[g].astype(BF16)
                out = _dot(jnp.concatenate([rp_ref[slot, j * L:(j + 1) * L, sl], gm_ref[slot, j, g]],
                                           axis=0), s2)
                y_ref[pl.ds(r0, L), sl] = out[0:L, :] + yi_ref[slot, j * L:(j + 1) * L, sl]
                state_ref[g] = out[L:, :] + hm_ref[slot, j, g]
        return step

    def process(chunks, slot, pending):
        pending = list(pending)

        def tick():
            if pending:
                pending.pop(0)()

        n = range(len(chunks))
        r0 = [_scaled(c, L) for c in chunks]
        at, rt, bt, kt, bh, kh, vv, wl = [], [], [], [], [], [], [], []
        for r in r0:
            lw = logw_ref[pl.ds(r, L), :]
            g_inc = _split_dot_lhs(tril, lw)
            e_in = jnp.exp(g_inc)
            e_neg = jnp.exp(-g_inc)
            e_last = jnp.exp(g_inc[L - 1:L, :] - g_inc)
            kn = kn_ref[pl.ds(r, L), :]
            b = b_ref[pl.ds(r, L), :]
            k2 = k2_ref[pl.ds(r, L), :]
            at.append((-kn * jnp.exp(g_inc - lw)).astype(BF16))
            rt.append(r_ref[pl.ds(r, L), :] * e_in)
            bt.append((b * e_neg).astype(BF16))
            kt.append((k2 * e_neg).astype(BF16))
            bh.append((b * e_last).astype(BF16))
            kh.append((k2 * e_last).astype(BF16))
            vv.append(v_ref[pl.ds(r, L), :])
            wl.append(e_in[L - 1:L, :])
        tick()
        sc = [_dot_nt(stack(at[i], rt[i]),
                      jnp.concatenate([expand(bt[i]), expand(kt[i])], axis=0)) for i in n]
        tick()
        a_ab = [jnp.where(strict, s[0:L, 0:LANES], 0.0) for s in sc]
        a_ak = [jnp.where(strict, s[0:L, LANES:2 * LANES], 0.0) for s in sc]
        m_rb = [jnp.where(incl, s[L:2 * L, 0:LANES], 0.0) for s in sc]
        m_rk = [jnp.where(incl, s[L:2 * L, LANES:2 * LANES], 0.0) for s in sc]
        p = [eye_p + a for a in a_ab]
        apow = [bdot(a, a) for a in a_ab]
        tick()
        for _ in range(2):
            st = [bdot(stack(p[i], apow[i]), apow[i]) for i in n]
            p = [p[i] + st[i][0:L, :] for i in n]
            apow = [s[L:2 * L, :] for s in st]
            tick()
        p = [p[i] + bdot(p[i], apow[i]) for i in n]
        tick()
        mp = [bdot(m_rb[i], p[i]) for i in n]
        tick()
        pm = [stack(p[i], mp[i]) for i in n]
        st = [bdot(pm[i], a_ak[i]) for i in n]
        pq = [stack(st[i][0:L, :], st[i][L:2 * L, :] + m_rk[i]) for i in n]
        tick()
        st = [_dot(pm[i], expand(at[i])) for i in n]
        atp = [s[0:L, :].astype(BF16) for s in st]
        for i in n:
            rp_ref[slot, i * L:(i + 1) * L, :] = (rt[i] + st[i][L:2 * L, :]).astype(BF16)
        tick()
        st = [_dot(pq[i], expand(vv[i])) for i in n]
        uv = [s[0:L, :].astype(BF16) for s in st]
        for i in n:
            yi_ref[slot, i * L:(i + 1) * L, :] = st[i][L:2 * L, :]
        tick()
        for i in n:
            for g in range(N_GROUPS):
                sl = slice(g * LANES, (g + 1) * LANES)
                rhs = jnp.concatenate(
                    [jnp.concatenate([atp[i][:, sl], uv[i][:, sl]], axis=1),
                     jnp.concatenate([jnp.zeros((L, LANES), BF16), vv[i][:, sl]], axis=1)], axis=0)
                out = _dot_tn(jnp.concatenate([bh[i][:, sl], kh[i][:, sl]], axis=0), rhs)
                gm = out[:, 0:LANES] * mpair_ref[...] + jnp.where(diag, wl[i][:, sl], 0.0)
                gm_ref[slot, i, g] = gm.astype(BF16)
                hm_ref[slot, i, g] = out[:, LANES:2 * LANES] * mpair_ref[...]
            tick()
        while pending:
            pending.pop(0)()

    n_chunks = tm // L
    n_full = n_chunks // WKV_GROUP
    rem = n_chunks % WKV_GROUP
    process(list(range(WKV_GROUP)), 0, [])

    def trip(t, carry):
        slot = t % 2
        process([t * WKV_GROUP + u for u in range(WKV_GROUP)], slot,
                [advance((t - 1) * WKV_GROUP + u, 1 - slot, u) for u in range(WKV_GROUP)])
        return carry

    lax.fori_loop(1, n_full, trip, 0)
    last = (n_full - 1) % 2
    process([n_full * WKV_GROUP + u for u in range(rem)], 1 - last,
            [advance((n_full - 1) * WKV_GROUP + u, last, u) for u in range(WKV_GROUP)])
    process([], 1 - last, [advance(n_full * WKV_GROUP + u, 1 - last, u) for u in range(rem)])

    y = y_ref[...]
    mean = _split_dot(y, hsum_ref[...]) * (1.0 / RWKV_HEAD)
    cen = y - mean
    var = _split_dot(cen * cen, hsum_ref[...]) * (1.0 / RWKV_HEAD)
    yn = cen * lax.rsqrt(var + RWKV_GN_EPS) * lng_ref[...] + lnb_ref[...]
    o_ref[0] = ((yn + bonus_ref[...]) * gate_ref[...]).astype(BF16)


def _rwkv(pr, mu, w0, a0, w_lora, g2, k_k, k_a, r_k, lnx_g, lnx_b):
    bsz, t_len, _ = pr.shape
    tm = ROW_TILE
    nt = t_len // tm
    assert (tm // WKV_CHUNK) // WKV_GROUP >= 1 and tm % WKV_CHUNK == 0
    group_rows = WKV_GROUP * WKV_CHUNK

    lane_head = jnp.arange(D_R) // RWKV_HEAD
    hsum = (lane_head[:, None] == lane_head[None, :]).astype(BF16)
    row_head = jnp.arange(LANES) // WKV_CHUNK
    mexp = (row_head[:, None] == lane_head[None, :]).astype(BF16)
    mbd = (row_head[:, None] == row_head[None, :]).astype(BF16)
    pair_head = jnp.arange(LANES) // RWKV_HEAD
    mpair = (pair_head[:, None] == pair_head[None, :]).astype(F32)
    tril = jnp.tril(jnp.ones((WKV_CHUNK, WKV_CHUNK), F32)).astype(BF16)

    tile_f32 = pltpu.VMEM((tm, D_R), F32)
    tile_bf16 = pltpu.VMEM((tm, D_R), BF16)
    return pl.pallas_call(
        _rwkv_kernel,
        out_shape=jax.ShapeDtypeStruct((bsz, t_len, D_R), BF16),
        grid=(bsz, nt),
        in_specs=[
            pl.BlockSpec((1, tm, RWKV_COLS), lambda b, i: (b, i, 0)),
            _const_spec((1, RWKV_COLS)),
            _const_spec((1, D_R)),
            _const_spec((1, D_R)),
            _const_spec((LORA_W + LORA_A, 2 * D_R)),
            _const_spec((LORA_G, D_R)),
            _const_spec((1, D_R)),
            _const_spec((1, D_R)),
            _const_spec((1, D_R)),
            _const_spec((1, D_R)),
            _const_spec((1, D_R)),
            _const_spec((D_R, D_R)),
            _const_spec((LANES, D_R)),
            _const_spec((LANES, LANES)),
            _const_spec((LANES, LANES)),
            _const_spec((WKV_CHUNK, WKV_CHUNK)),
        ],
        out_specs=pl.BlockSpec((1, tm, D_R), lambda b, i: (b, i, 0)),
        scratch_shapes=[
            pltpu.VMEM((1, RWKV_COLS), F32),
            pltpu.VMEM((N_GROUPS, LANES, LANES), F32),
            tile_f32, tile_f32, tile_f32, tile_f32, tile_f32, tile_bf16,
            pltpu.VMEM((2, group_rows, D_R), BF16),
            pltpu.VMEM((2, group_rows, D_R), F32),
            pltpu.VMEM((2, WKV_GROUP, N_GROUPS, LANES, LANES), BF16),
            pltpu.VMEM((2, WKV_GROUP, N_GROUPS, LANES, LANES), F32),
            tile_f32, tile_f32, tile_f32,
        ],
        compiler_params=pltpu.CompilerParams(
            dimension_semantics=("arbitrary", "arbitrary"),
            vmem_limit_bytes=VMEM_LIMIT_BYTES),
        name="rwkv7",
    )(pr, mu, w0, a0, w_lora, g2, k_k, k_a, r_k, lnx_g, lnx_b, hsum, mexp, mbd, mpair, tril)


def _rope_tables(t_len):
    half = HEAD_DIM // 2
    inv = ROPE_THETA ** (-jnp.arange(half, dtype=F32) / half)
    ang = jnp.arange(t_len, dtype=F32)[:, None] * inv[None, :]
    cos = jnp.cos(ang)
    sin = jnp.sin(ang)
    reps = LANES // HEAD_DIM
    cos_t = jnp.tile(jnp.concatenate([cos, cos], axis=1), (1, reps))
    sin_t = jnp.tile(jnp.concatenate([-sin, sin], axis=1), (1, reps))
    return cos_t, sin_t


def kernel(x, meta_tokens, norm_mix, norm_ffn, norm_final, ev_w_in, ev_conv_a, ev_ln_a_g, ev_ln_a_b, ev_conv_b, ev_w_out, od_w_in, od_sinks, od_mu, od_w0, od_w2, od_a0, od_a2, od_g2, od_k_k, od_k_a, od_r_k, od_lnx_g, od_lnx_b, od_w_out, ff_w_up, ff_conv, ff_conv_b, ff_w_down):
    bsz, seq, d = x.shape
    depth = norm_mix.shape[0]
    t_len = N_META + seq
    row = lambda v: v.reshape(1, -1).astype(F32)
    cos_t, sin_t = _rope_tables(t_len)
    ff_up = ff_w_up.astype(BF16)
    ff_down = ff_w_down.astype(BF16)
    ff_bias = ff_conv_b.reshape(depth, 1, D_FF)
    h = x
    for i in range(depth):
        j = i // 2
        mixer_out = None
        if i % 2 == 0:
            h = _even_mixer(h, row(norm_mix[i]), ev_w_in[j].astype(BF16),
                            jnp.repeat(ev_conv_a[j], SUBLANES, axis=0),
                            row(ev_ln_a_g[j]), row(ev_ln_a_b[j]), ev_conv_b[j],
                            ev_w_out[j].astype(BF16),
                            meta=meta_tokens.astype(F32) if i == 0 else None)
        else:
            h2d = h.reshape(bsz * t_len, d)
            w_in = od_w_in[j].astype(BF16)
            y_att = _attention(h, row(norm_mix[i]), w_in[:, :ATT_COLS], cos_t, sin_t,
                               row(od_sinks[j]))
            pr = _rwkv_in_proj(h2d, row(norm_mix[i]), w_in[:, ATT_COLS:])
            zeros = jnp.zeros((LORA_W, D_R), F32)
            w_lora = jnp.concatenate(
                [jnp.concatenate([od_w2[j], zeros], axis=1),
                 jnp.concatenate([zeros, od_a2[j]], axis=1)], axis=0).astype(BF16)
            y_rwkv = _rwkv(pr.reshape(bsz, t_len, RWKV_COLS), row(od_mu[j]), row(od_w0[j]),
                           row(od_a0[j]), w_lora, od_g2[j].astype(BF16), row(od_k_k[j]),
                           row(od_k_a[j]), row(od_r_k[j]), row(od_lnx_g[j]), row(od_lnx_b[j]))
            mixer_out = (y_att, y_rwkv, od_w_out[j].astype(BF16))
        last = i == depth - 1
        h = _ffn(h, row(norm_ffn[i]), ff_up, ff_conv, ff_bias, ff_down, row(norm_final),
                 layer=i, final_norm=last, mixer_out=mixer_out, drop_meta=last)
    return h
```
